```python
import math
import jax, jax.numpy as jnp
from jax import lax
import numpy as np

D_MODEL = 1024
BATCH = 4
SEQ = 8192
DEPTH = 1
DEC_BATCH = 32
DEC_SEQ = 8
PAST_LEN = 16384
PAGE_SIZE = 128

N_HEADS = 8
HEAD_DIM = 64
V_DIM = 2 * HEAD_DIM
QK_WIDTH = 2 * N_HEADS * HEAD_DIM
ATT_WIDTH = N_HEADS * V_DIM
SCALE = HEAD_DIM ** -0.5
Q_BLOCK = 128
ROPE_THETA = 10000.0
GMLP_WIDTH = D_MODEL
SGU_GROUPS = 4
CHUNK = 128
N_GROUPS = 4
EXPERTS_PER_GROUP = 8
N_EXPERTS = N_GROUPS * EXPERTS_PER_GROUP
TOP_K = 2
D_EXPERT = D_MODEL // 2
MOE_BLOCK = 128
EPS = 1e-6
IN_COLS = 2 * QK_WIDTH + ATT_WIDTH + 2 * GMLP_WIDTH + 2 * D_MODEL

kernel_name = "hybrid_diffattn_sgu_hmoe_step"


def rms_norm(x, g):
    xf = x.astype(jnp.float32)
    y = xf * lax.rsqrt(jnp.mean(xf * xf, axis=-1, keepdims=True) + EPS)
    return (y * g.astype(jnp.float32)).astype(x.dtype)


def layer_norm(x, g, b):
    xf = x.astype(jnp.float32)
    mu = jnp.mean(xf, axis=-1, keepdims=True)
    var = jnp.mean(jnp.square(xf - mu), axis=-1, keepdims=True)
    y = (xf - mu) * lax.rsqrt(var + EPS)
    return (y * g.astype(jnp.float32) + b.astype(jnp.float32)).astype(x.dtype)


def rope(x, pos):
    hd = x.shape[-1]
    inv = 1.0 / (ROPE_THETA ** (jnp.arange(0, hd, 2, dtype=jnp.float32) / hd))
    ang = pos[:, None] * inv[None, :]
    cos = jnp.cos(ang)[:, None, :]
    sin = jnp.sin(ang)[:, None, :]
    xf = x.astype(jnp.float32)
    x1, x2 = xf[..., : hd // 2], xf[..., hd // 2:]
    return jnp.concatenate([x1 * cos - x2 * sin, x2 * cos + x1 * sin], axis=-1).astype(x.dtype)


def diff_lambda(lq1, lk1, lq2, lk2, lam_init):
    f = jnp.float32
    return (jnp.exp(jnp.sum(lq1.astype(f) * lk1.astype(f)))
            - jnp.exp(jnp.sum(lq2.astype(f) * lk2.astype(f))) + lam_init)


def diff_attn_prompt(q, k, v, lam):
    B, S = q.shape[0], q.shape[1]
    nb = S // Q_BLOCK
    qb = q.reshape(B, nb, Q_BLOCK, 2 * N_HEADS, HEAD_DIM).swapaxes(0, 1)
    kpos = jnp.arange(S)

    def one_block(args):
        qi, i = args
        s = jnp.einsum('bqnd,bknd->bnqk', qi, k, preferred_element_type=jnp.float32) * SCALE
        qpos = i * Q_BLOCK + jnp.arange(Q_BLOCK)
        mask = kpos[None, :] <= qpos[:, None]
        p = jax.nn.softmax(jnp.where(mask, s, -jnp.inf), axis=-1)
        p = p.reshape(B, N_HEADS, 2, Q_BLOCK, S)
        a = (p[:, :, 0] - lam * p[:, :, 1]).astype(v.dtype)
        return jnp.einsum('bhqk,bkhe->bqhe', a, v)

    o = lax.map(one_block, (qb, jnp.arange(nb)))
    return o.swapaxes(0, 1).reshape(B, S, N_HEADS, V_DIM)


def diff_attn_sample(q, k_new, v_new, k_cache, v_cache, page_table, lam):
    past = page_table.shape[1] * PAGE_SIZE
    ds = q.shape[1]
    kidx = jnp.arange(past + ds)
    qidx = jnp.arange(ds)
    mask = (kidx[None, :] < past) | (kidx[None, :] - past <= qidx[:, None])

    def one_seq(args):
        qi, kn, vn, pt = args
        kp = k_cache[pt].reshape(past, 2 * N_HEADS, HEAD_DIM)
        vp = v_cache[pt].reshape(past, N_HEADS, V_DIM)
        ka = jnp.concatenate([kp, kn.astype(kp.dtype)], axis=0)
        va = jnp.concatenate([vp, vn.astype(vp.dtype)], axis=0)
        s = jnp.einsum('qnd,knd->nqk', qi, ka, preferred_element_type=jnp.float32) * SCALE
        p = jax.nn.softmax(jnp.where(mask, s, -jnp.inf), axis=-1)
        p = p.reshape(N_HEADS, 2, ds, past + ds)
        a = (p[:, 0] - lam * p[:, 1]).astype(va.dtype)
        return jnp.einsum('hqk,khe->qhe', a, va)

    return lax.map(one_seq, (q, k_new, v_new, page_table))


def sgu_prompt(u, vn, w_s, b_s):
    B, S, W = u.shape
    nc = S // CHUNK
    vr = vn.reshape(B, nc, CHUNK, SGU_GROUPS, W // SGU_GROUPS)
    ws = jnp.tril(w_s).astype(vn.dtype)
    mixed = jnp.einsum('gts,bcsgd->bctgd', ws, vr) + b_s.T.astype(vn.dtype)[:, :, None]
    return u * mixed.reshape(B, S, W)


def sgu_sample(u, vn, w_s, b_s):
    DB, ds, W = u.shape
    vr = vn.reshape(DB, ds, SGU_GROUPS, W // SGU_GROUPS)
    ws = jnp.tril(w_s)[:, :ds, :ds].astype(vn.dtype)
    mixed = jnp.einsum('gts,bsgd->btgd', ws, vr) + b_s[:, :ds].T.astype(vn.dtype)[:, :, None]
    return u * mixed.reshape(DB, ds, W)


def mixer_sublayer(x, pos, norm_g, w_in, lam, lam_init, subln_g, ln_g, ln_b, w_s, b_s,
                   w_att_br, w_sgu_br, w_out, attn_core, sgu_mix):
    lead = x.shape[:-1]
    h = rms_norm(x, norm_g)
    z = h @ w_in
    q, k, v, u, vg, ga, gb = jnp.split(
        z, [QK_WIDTH, 2 * QK_WIDTH, 2 * QK_WIDTH + ATT_WIDTH,
            2 * QK_WIDTH + ATT_WIDTH + GMLP_WIDTH,
            2 * QK_WIDTH + ATT_WIDTH + 2 * GMLP_WIDTH,
            2 * QK_WIDTH + ATT_WIDTH + 2 * GMLP_WIDTH + D_MODEL], axis=-1)
    q = rope(q.reshape(*lead, 2 * N_HEADS, HEAD_DIM), pos)
    k = rope(k.reshape(*lead, 2 * N_HEADS, HEAD_DIM), pos)
    v = v.reshape(*lead, N_HEADS, V_DIM)
    o = attn_core(q, k, v, lam)
    o_att = (rms_norm(o, subln_g) * (1.0 - lam_init)).reshape(*lead, ATT_WIDTH)
    vn = layer_norm(jax.nn.gelu(vg), ln_g, ln_b)
    o_sgu = sgu_mix(jax.nn.gelu(u), vn, w_s, b_s)
    m = jax.nn.sigmoid(ga) * (o_att @ w_att_br) + jax.nn.sigmoid(gb) * (o_sgu @ w_sgu_br)
    return x + m @ w_out, k, v, vn


def hier_moe(h, w_rg, b_rg, w_re, b_re, w_gate, w_up, w_down):
    T, D = h.shape
    gl = (h @ w_rg).astype(jnp.float32) + b_rg.astype(jnp.float32)
    g_sel = jnp.argmax(gl, axis=-1)
    p_g = jnp.take_along_axis(jax.nn.softmax(gl, axis=-1), g_sel[:, None], axis=-1)[:, 0]
    el = jnp.einsum('td,gde->tge', h, w_re).astype(jnp.float32) + b_re.astype(jnp.float32)
    el_sel = jnp.take_along_axis(el, g_sel[:, None, None], axis=1)[:, 0]
    top_v, top_i = lax.top_k(el_sel, TOP_K)
    wts = jax.nn.softmax(top_v, axis=-1) * p_g[:, None]
    e_idx = g_sel[:, None].astype(jnp.int32) * EXPERTS_PER_GROUP + top_i.astype(jnp.int32)

    TK = T * TOP_K
    flat_e = e_idx.reshape(TK)
    flat_t = jnp.repeat(jnp.arange(T, dtype=jnp.int32), TOP_K)
    flat_w = wts.reshape(TK)
    order = jnp.argsort(flat_e)
    se = flat_e[order]
    counts = jnp.bincount(flat_e, length=N_EXPERTS)
    padded = (counts + MOE_BLOCK - 1) // MOE_BLOCK * MOE_BLOCK
    pad_end = jnp.cumsum(padded)
    pad_start = pad_end - padded
    start = jnp.cumsum(counts) - counts
    dest = pad_start[se] + jnp.arange(TK) - start[se]
    n_blocks = (TK + N_EXPERTS * (MOE_BLOCK - 1) + MOE_BLOCK - 1) // MOE_BLOCK
    n_rows = n_blocks * MOE_BLOCK
    row_tok = jnp.full((n_rows,), T, jnp.int32).at[dest].set(flat_t[order])
    row_w = jnp.zeros((n_rows,), h.dtype).at[dest].set(flat_w[order].astype(h.dtype))
    blk_e = jnp.minimum(jnp.searchsorted(pad_end, jnp.arange(n_blocks) * MOE_BLOCK, side='right'),
                        N_EXPERTS - 1)
    h_pad = jnp.concatenate([h, jnp.zeros((1, D), h.dtype)], axis=0)
    xb = h_pad[row_tok].reshape(n_blocks, MOE_BLOCK, D)

    def expert_block(args):
        xi, e = args
        return (jax.nn.silu(xi @ w_gate[e]) * (xi @ w_up[e])) @ w_down[e]

    yb = lax.map(expert_block, (xb, blk_e)).reshape(n_rows, D)
    out = jnp.zeros((T + 1, D), h.dtype).at[row_tok].add(yb * row_w[:, None])
    return out[:T]


def channel_sublayer(x, norm_g, w_rg, b_rg, w_re, b_re, w_gate, w_up, w_down):
    h = rms_norm(x, norm_g).reshape(-1, D_MODEL)
    return x + hier_moe(h, w_rg, b_rg, w_re, b_re, w_gate, w_up, w_down).reshape(x.shape)


def setup_inputs(seed: int = 0) -> dict:
    key = jax.random.key(seed)
    ks = jax.random.split(key, 32)
    f = jnp.float32

    def nrm(k, shape, scale):
        return jax.random.normal(k, shape, f) * scale

    n_pages = PAST_LEN // PAGE_SIZE
    n_phys = (DEC_BATCH * n_pages * 5) // 4
    page_table = jax.random.permutation(ks[4], n_phys)[: DEC_BATCH * n_pages].reshape(
        DEC_BATCH, n_pages).astype(jnp.int32)
    return {
        "x_prompt": nrm(ks[0], (BATCH, SEQ, D_MODEL), 1.0),
        "x_sample": nrm(ks[1], (DEC_BATCH, DEC_SEQ, D_MODEL), 1.0),
        "cache_k": nrm(ks[2], (DEPTH, n_phys, PAGE_SIZE, 2 * N_HEADS, HEAD_DIM), 1.0),
        "cache_v": nrm(ks[3], (DEPTH, n_phys, PAGE_SIZE, N_HEADS, V_DIM), 1.0),
        "page_table": page_table,
        "norm1_g": 1.0 + nrm(ks[5], (DEPTH, D_MODEL), 0.02),
        "w_in": nrm(ks[6], (DEPTH, D_MODEL, IN_COLS), D_MODEL ** -0.5),
        "lambda_q1": nrm(ks[7], (DEPTH, HEAD_DIM), 0.1),
        "lambda_k1": nrm(ks[8], (DEPTH, HEAD_DIM), 0.1),
        "lambda_q2": nrm(ks[9], (DEPTH, HEAD_DIM), 0.1),
        "lambda_k2": nrm(ks[10], (DEPTH, HEAD_DIM), 0.1),
        "subln_g": 1.0 + nrm(ks[11], (DEPTH, V_DIM), 0.02),
        "sgu_ln_g": 1.0 + nrm(ks[12], (DEPTH, GMLP_WIDTH), 0.02),
        "sgu_ln_b": nrm(ks[13], (DEPTH, GMLP_WIDTH), 0.02),
        "sgu_w": nrm(ks[14], (DEPTH, SGU_GROUPS, CHUNK, CHUNK), CHUNK ** -0.5),
        "sgu_b": 1.0 + nrm(ks[15], (DEPTH, SGU_GROUPS, CHUNK), 0.1),
        "w_att_branch": nrm(ks[16], (DEPTH, ATT_WIDTH, D_MODEL), ATT_WIDTH ** -0.5),
        "w_sgu_branch": nrm(ks[17], (DEPTH, GMLP_WIDTH, D_MODEL), GMLP_WIDTH ** -0.5),
        "w_out": nrm(ks[18], (DEPTH, D_MODEL, D_MODEL), D_MODEL ** -0.5),
        "norm2_g": 1.0 + nrm(ks[19], (DEPTH, D_MODEL), 0.02),
        "w_router_group": nrm(ks[20], (DEPTH, D_MODEL, N_GROUPS), D_MODEL ** -0.5),
        "b_router_group": nrm(ks[21], (DEPTH, N_GROUPS), 0.01),
        "w_router_expert": nrm(ks[22], (DEPTH, N_GROUPS, D_MODEL, EXPERTS_PER_GROUP), D_MODEL ** -0.5),
        "b_router_expert": nrm(ks[23], (DEPTH, N_GROUPS, EXPERTS_PER_GROUP), 0.01),
        "w_expert_gate": nrm(ks[24], (DEPTH, N_EXPERTS, D_MODEL, D_EXPERT), D_MODEL ** -0.5),
        "w_expert_up": nrm(ks[25], (DEPTH, N_EXPERTS, D_MODEL, D_EXPERT), D_MODEL ** -0.5),
        "w_expert_down": nrm(ks[26], (DEPTH, N_EXPERTS, D_EXPERT, D_MODEL), D_EXPERT ** -0.5),
        "final_g": 1.0 + nrm(ks[27], (D_MODEL,), 0.02),
    }


def reference(x_prompt, x_sample, cache_k, cache_v, page_table, norm1_g, w_in,
              lambda_q1, lambda_k1, lambda_q2, lambda_k2, subln_g, sgu_ln_g, sgu_ln_b,
              sgu_w, sgu_b, w_att_branch, w_sgu_branch, w_out, norm2_g,
              w_router_group, b_router_group, w_router_expert, b_router_expert,
              w_expert_gate, w_expert_up, w_expert_down, final_g):
    past = page_table.shape[1] * PAGE_SIZE
    pos_p = jnp.arange(x_prompt.shape[1], dtype=jnp.float32)
    pos_s = jnp.arange(x_sample.shape[1], dtype=jnp.float32) + float(past)
    xp, xs = x_prompt, x_sample
    kp_l, vp_l, ks_l, vs_l, gs_l = [], [], [], [], []
    for layer in range(DEPTH):
        lam_init = 0.8 - 0.6 * math.exp(-0.3 * layer)
        lam = diff_lambda(lambda_q1[layer], lambda_k1[layer], lambda_q2[layer],
                          lambda_k2[layer], lam_init)
        shared = (norm1_g[layer], w_in[layer], lam, lam_init, subln_g[layer],
                  sgu_ln_g[layer], sgu_ln_b[layer], sgu_w[layer], sgu_b[layer],
                  w_att_branch[layer], w_sgu_branch[layer], w_out[layer])
        ck, cv = cache_k[layer], cache_v[layer]

        def sample_attn(q, k, v, lm, ck=ck, cv=cv):
            return diff_attn_sample(q, k, v, ck, cv, page_table, lm)

        xp, kp, vp, _ = mixer_sublayer(xp, pos_p, *shared, diff_attn_prompt, sgu_prompt)
        xs, ksn, vsn, gsn = mixer_sublayer(xs, pos_s, *shared, sample_attn, sgu_sample)
        moe_w = (norm2_g[layer], w_router_group[layer], b_router_group[layer],
                 w_router_expert[layer], b_router_expert[layer], w_expert_gate[layer],
                 w_expert_up[layer], w_expert_down[layer])
        xp = channel_sublayer(xp, *moe_w)
        xs = channel_sublayer(xs, *moe_w)
        kp_l.append(kp); vp_l.append(vp); ks_l.append(ksn); vs_l.append(vsn); gs_l.append(gsn)
    y_prompt = rms_norm(xp, final_g)
    y_sample = rms_norm(xs, final_g)
    new_k_prompt = jnp.stack(kp_l)
    new_v_prompt = jnp.stack(vp_l)
    new_k_sample = jnp.stack(ks_l)
    new_v_sample = jnp.stack(vs_l)
    new_sgu_v_sample = jnp.stack(gs_l)
    return (y_prompt, y_sample, new_k_prompt, new_v_prompt, new_k_sample, new_v_sample, new_sgu_v_sample)
```

```python
import functools
import math

import jax
import jax.numpy as jnp
from jax import lax
from jax.experimental import pallas as pl
from jax.experimental.pallas import tpu as pltpu

F32 = jnp.float32
BF16 = jnp.bfloat16

EPS = 1e-6
ROPE_THETA = 10000.0
TOP_K = 2
LANES = 128
VMEM_LIMIT_BYTES = 56 * 1024 * 1024

PROJ_ROWS = 512
MIX_ROWS = 256
ATT_BLOCK = 512
PAGES_PER_STEP = 8
MOE_ROWS = 256
OUT_ROWS = 512


def _params(n_axes):
    return pltpu.CompilerParams(dimension_semantics=("arbitrary",) * n_axes,
                                vmem_limit_bytes=VMEM_LIMIT_BYTES)


def _rms(x, g):
    return (x * lax.rsqrt(jnp.mean(x * x, axis=-1, keepdims=True) + EPS)) * g


def _log2(n):
    assert n > 0 and n & (n - 1) == 0, n
    return n.bit_length() - 1


def _rope_tables(pos, head_dim):
    inv = 1.0 / (ROPE_THETA ** (jnp.arange(0, head_dim, 2, dtype=F32) / head_dim))
    ang = pos[:, None] * inv[None, :]
    cos, sin = jnp.cos(ang), jnp.sin(ang)
    zero = jnp.zeros_like(sin)
    reps = LANES // head_dim
    cos_t = jnp.concatenate([cos, cos] * reps, axis=-1)
    sin_lo = jnp.concatenate([-sin, zero] * reps, axis=-1)
    sin_hi = jnp.concatenate([zero, sin] * reps, axis=-1)
    return cos_t, sin_lo, sin_hi


def _qkv_kernel(x_ref, g_ref, w_ref, cos_ref, slo_ref, shi_ref,
                kout_ref, vout_ref, q_ref, kb_ref, vb_ref, *, qk_width, half, scale):
    h = _rms(x_ref[...], g_ref[...])
    z = jnp.dot(h.astype(BF16), w_ref[...], preferred_element_type=F32)
    cos, slo, shi = cos_ref[...], slo_ref[...], shi_ref[...]

    def rope(blk):
        return (blk * cos + pltpu.roll(blk, LANES - half, 1) * slo
                + pltpu.roll(blk, half, 1) * shi)

    for j in range(qk_width // LANES):
        cols = slice(j * LANES, (j + 1) * LANES)
        q_ref[:, cols] = (rope(z[:, cols]) * scale).astype(q_ref.dtype)
        kr = rope(z[:, qk_width + j * LANES: qk_width + (j + 1) * LANES])
        kout_ref[:, cols] = kr
        kb_ref[:, cols] = kr.astype(BF16)
    v = z[:, 2 * qk_width:]
    vout_ref[...] = v
    vb_ref[...] = v.astype(BF16)


def _qkv_proj(x, g, w_qkv, tables, *, rows, tiles_per_seq, qk_width, att_width, head_dim,
              q_dtype, name):
    t, d = x.shape
    n_tiles = t // rows
    row_spec = lambda w: pl.BlockSpec((rows, w), lambda i: (i, 0))
    tab_spec = pl.BlockSpec((rows, LANES), lambda i: (i % tiles_per_seq, 0))
    full = lambda a: pl.BlockSpec(a.shape, lambda i: (0,) * a.ndim)
    kern = functools.partial(_qkv_kernel, qk_width=qk_width, half=head_dim // 2,
                             scale=head_dim ** -0.5)
    return pl.pallas_call(
        kern,
        grid=(n_tiles,),
        in_specs=[row_spec(d), full(g), full(w_qkv), tab_spec, tab_spec, tab_spec],
        out_specs=[row_spec(qk_width), row_spec(att_width), row_spec(qk_width),
                   row_spec(qk_width), row_spec(att_width)],
        out_shape=[jax.ShapeDtypeStruct((t, qk_width), F32),
                   jax.ShapeDtypeStruct((t, att_width), F32),
                   jax.ShapeDtypeStruct((t, qk_width), q_dtype),
                   jax.ShapeDtypeStruct((t, qk_width), BF16),
                   jax.ShapeDtypeStruct((t, att_width), BF16)],
        compiler_params=_params(1),
        name=name,
    )(x, g, w_qkv, *tables)


def _diff_lambda(lq1_ref, lk1_ref, lq2_ref, lk2_ref, lam_init):
    s1 = jnp.sum(lq1_ref[...] * lk1_ref[...], axis=-1, keepdims=True)
    s2 = jnp.sum(lq2_ref[...] * lk2_ref[...], axis=-1, keepdims=True)
    return jnp.exp(s1) - jnp.exp(s2) + lam_init


def _subln(o1, o2, lam, g, lam_init):
    d = o1 - lam * o2
    return _rms(d, g) * (1.0 - lam_init)


def _prompt_attn_kernel(qi_ref, ki_ref, q_ref, k_ref, v_ref, lq1_ref, lk1_ref, lq2_ref,
                        lk2_ref, g_ref, o_ref, q2_ref, m_ref, l_ref, acc_ref,
                        *, blk, head_dim, lam_init):
    p = pl.program_id(2)
    qi, ki = qi_ref[p], ki_ref[p]
    rows = 2 * blk

    @pl.when(ki == 0)
    def _init():
        q = q_ref[...]
        lane = lax.broadcasted_iota(jnp.int32, q.shape, 1)
        zero = jnp.zeros_like(q)
        q2_ref[:blk, :] = jnp.where(lane < head_dim, q, zero)
        q2_ref[blk:, :] = jnp.where(lane >= head_dim, q, zero)
        m_ref[...] = jnp.full(m_ref.shape, -jnp.inf, F32)
        l_ref[...] = jnp.zeros(l_ref.shape, F32)
        acc_ref[...] = jnp.zeros(acc_ref.shape, F32)

    def step(masked):
        s = lax.dot_general(q2_ref[...], k_ref[...], (((1,), (1,)), ((), ())),
                            preferred_element_type=F32)
        if masked:
            r = lax.broadcasted_iota(jnp.int32, s.shape, 0)
            c = lax.broadcasted_iota(jnp.int32, s.shape, 1)
            r = jnp.where(r >= blk, r - blk, r)
            s = jnp.where(c <= r, s, -jnp.inf)
        m_prev = m_ref[...]
        m_new = jnp.maximum(m_prev, jnp.max(s, axis=1, keepdims=True))
        alpha = jnp.exp(m_prev - m_new)
        pexp = jnp.exp(s - jnp.concatenate([m_new] * (blk // LANES), axis=1))
        l_ref[...] = alpha * l_ref[...] + jnp.sum(pexp, axis=1, keepdims=True)
        acc_ref[...] = alpha * acc_ref[...] + jnp.dot(
            pexp.astype(BF16), v_ref[...], preferred_element_type=F32)
        m_ref[...] = m_new

    @pl.when(ki < qi)
    def _off_diagonal():
        step(False)

    @pl.when(ki == qi)
    def _diagonal():
        step(True)
        o = acc_ref[...] / l_ref[...]
        lam = _diff_lambda(lq1_ref, lk1_ref, lq2_ref, lk2_ref, lam_init)
        o_ref[...] = _subln(o[:blk], o[blk:], lam, g_ref[...], lam_init).astype(o_ref.dtype)


def _prompt_attention(qb, kb, vb, lam_vecs, subln_g, *, batch, seq, n_heads, head_dim,
                      lam_init, name):
    blk = min(ATT_BLOCK, seq)
    nq = seq // blk
    pairs = [(i, j) for i in range(nq) for j in range(i + 1)]
    qi_map = jnp.asarray([a for a, _ in pairs], jnp.int32)
    ki_map = jnp.asarray([b for _, b in pairs], jnp.int32)
    v_dim = 2 * head_dim
    q_spec = pl.BlockSpec((blk, v_dim), lambda b, h, p, qi, ki: (b * nq + qi[p], h))
    kv_spec = pl.BlockSpec((blk, v_dim), lambda b, h, p, qi, ki: (b * nq + ki[p], h))
    vec = lambda a: pl.BlockSpec(a.shape, lambda b, h, p, qi, ki: (0, 0))
    kern = functools.partial(_prompt_attn_kernel, blk=blk, head_dim=head_dim,
                             lam_init=lam_init)
    return pl.pallas_call(
        kern,
        grid_spec=pltpu.PrefetchScalarGridSpec(
            num_scalar_prefetch=2,
            grid=(batch, n_heads, len(pairs)),
            in_specs=[q_spec, kv_spec, kv_spec] + [vec(a) for a in lam_vecs] + [vec(subln_g)],
            out_specs=q_spec,
            scratch_shapes=[pltpu.VMEM((2 * blk, v_dim), BF16),
                            pltpu.VMEM((2 * blk, LANES), F32),
                            pltpu.VMEM((2 * blk, LANES), F32),
                            pltpu.VMEM((2 * blk, v_dim), F32)]),
        out_shape=jax.ShapeDtypeStruct((batch * seq, n_heads * v_dim), BF16),
        compiler_params=_params(3),
        name=name,
    )(qi_map, ki_map, qb, kb, vb, *lam_vecs, subln_g)


def _decode_attn_kernel(*refs, n_pages_step, page, ds, n_maps, head_dim, lam_init):
    pt_ref = refs[0]
    q_ref, knew_ref, vnew_ref = refs[1:4]
    k_refs = refs[4:4 + n_pages_step]
    v_refs = refs[4 + n_pages_step:4 + 2 * n_pages_step]
    lq1_ref, lk1_ref, lq2_ref, lk2_ref, g_ref = refs[4 + 2 * n_pages_step:9 + 2 * n_pages_step]
    o_ref = refs[9 + 2 * n_pages_step]
    qs_ref, kbuf_ref, vbuf_ref, m_ref, l_ref, acc_ref = refs[10 + 2 * n_pages_step:]
    del pt_ref
    j = pl.program_id(1)
    rows = n_maps * ds

    def update(s, v):
        m_prev = m_ref[...]
        m_new = jnp.maximum(m_prev, jnp.max(s, axis=1, keepdims=True))
        alpha = jnp.exp(m_prev - m_new)
        pexp = jnp.exp(s - jnp.concatenate([m_new] * (s.shape[1] // LANES), axis=1))
        l_ref[...] = alpha * l_ref[...] + jnp.sum(pexp, axis=1, keepdims=True)
        acc_ref[...] = (jnp.concatenate([alpha] * (acc_ref.shape[1] // LANES), axis=1)
                        * acc_ref[...]
                        + jnp.dot(pexp.astype(BF16), v, preferred_element_type=F32))
        m_ref[...] = m_new

    def scores(k):
        return lax.dot_general(qs_ref[...], k, (((1,), (1,)), ((), ())),
                               preferred_element_type=F32)

    @pl.when(j == 0)
    def _init():
        q = q_ref[...]
        qt = jnp.concatenate([q] * n_maps, axis=0)
        r = lax.broadcasted_iota(jnp.int32, qt.shape, 0)
        c = lax.broadcasted_iota(jnp.int32, qt.shape, 1)
        same = (lax.shift_right_logical(c, _log2(head_dim))
                == lax.shift_right_logical(r, _log2(ds)))
        qs_ref[...] = jnp.where(same, qt, jnp.zeros_like(qt)).astype(BF16)
        m_ref[...] = jnp.full(m_ref.shape, -jnp.inf, F32)
        l_ref[...] = jnp.zeros(l_ref.shape, F32)
        acc_ref[...] = jnp.zeros(acc_ref.shape, F32)
        s = scores(knew_ref[0].astype(BF16))
        qpos = lax.broadcasted_iota(jnp.int32, s.shape, 0) & (ds - 1)
        kpos = lax.broadcasted_iota(jnp.int32, s.shape, 1)
        s = jnp.where(kpos <= qpos, s, -jnp.inf)
        update(s, vnew_ref[0].astype(BF16))

    for i in range(n_pages_step):
        kbuf_ref[i * page:(i + 1) * page, :] = k_refs[i][0].astype(BF16)
        vbuf_ref[i * page:(i + 1) * page, :] = v_refs[i][0].astype(BF16)
    update(scores(kbuf_ref[...]), vbuf_ref[...])

    @pl.when(j == pl.num_programs(1) - 1)
    def _finish():
        lam = _diff_lambda(lq1_ref, lk1_ref, lq2_ref, lk2_ref, lam_init)
        inv = 1.0 / l_ref[...]
        v_dim = 2 * head_dim
        for h in range(n_maps // 2):
            cols = slice(h * v_dim, (h + 1) * v_dim)
            r1 = slice(2 * h * ds, (2 * h + 1) * ds)
            r2 = slice((2 * h + 1) * ds, (2 * h + 2) * ds)
            o1 = acc_ref[r1, cols] * inv[r1]
            o2 = acc_ref[r2, cols] * inv[r2]
            o_ref[:, cols] = _subln(o1, o2, lam, g_ref[...], lam_init).astype(o_ref.dtype)


def _decode_attention(q, k_new, v_new, cache_k, cache_v, page_ids, lam_vecs, subln_g, *,
                      dec_batch, ds, n_pages, n_maps, head_dim, lam_init, name):
    n_phys, page, width = cache_k.shape
    pps = math.gcd(PAGES_PER_STEP, n_pages)
    n_steps = n_pages // pps
    rows = n_maps * ds
    pad = lambda a: jnp.pad(a.reshape(dec_batch, ds, width), ((0, 0), (0, page - ds), (0, 0)))
    q_spec = pl.BlockSpec((ds, width), lambda b, j, pt: (b, 0))
    new_spec = pl.BlockSpec((1, page, width), lambda b, j, pt: (b, 0, 0))

    def page_spec(i):
        return pl.BlockSpec((1, page, width),
                            lambda b, j, pt: (pt[b * n_pages + j * pps + i], 0, 0))

    vec = lambda a: pl.BlockSpec(a.shape, lambda b, j, pt: (0, 0))
    kern = functools.partial(_decode_attn_kernel, n_pages_step=pps, page=page, ds=ds,
                             n_maps=n_maps, head_dim=head_dim, lam_init=lam_init)
    return pl.pallas_call(
        kern,
        grid_spec=pltpu.PrefetchScalarGridSpec(
            num_scalar_prefetch=1,
            grid=(dec_batch, n_steps),
            in_specs=([q_spec, new_spec, new_spec]
                      + [page_spec(i) for i in range(pps)]
                      + [page_spec(i) for i in range(pps)]
                      + [vec(a) for a in lam_vecs] + [vec(subln_g)]),
            out_specs=q_spec,
            scratch_shapes=[pltpu.VMEM((rows, width), BF16),
                            pltpu.VMEM((pps * page, width), BF16),
                            pltpu.VMEM((pps * page, width), BF16),
                            pltpu.VMEM((rows, LANES), F32),
                            pltpu.VMEM((rows, LANES), F32),
                            pltpu.VMEM((rows, width), F32)]),
        out_shape=jax.ShapeDtypeStruct((dec_batch * ds, width), F32),
        compiler_params=_params(2),
        name=name,
    )(page_ids, q, pad(k_new), pad(v_new), *([cache_k] * pps), *([cache_v] * pps),
      *lam_vecs, subln_g)


def _gelu_tanh(x):
    return 0.5 * x * (1.0 + jnp.tanh(math.sqrt(2.0 / math.pi) * (x + 0.044715 * (x * x * x))))


def _mix_kernel(x_ref, oatt_ref, g1_ref, w2_ref, lng_ref, lnb_ref, mixw_ref, mixb_ref,
                wa_ref, ws_ref, wo_ref, g2_ref, wr_ref, br_ref,
                x1_ref, h2_ref, route_ref, *maybe_vn_ref,
                width, n_sgu_groups, chunk, n_groups, experts_per_group):
    x = x_ref[...]
    h = _rms(x, g1_ref[...]).astype(BF16)
    z = jnp.dot(h, w2_ref[...], preferred_element_type=F32)
    u = _gelu_tanh(z[:, :width])
    vg = _gelu_tanh(z[:, width:2 * width])
    mu = jnp.mean(vg, axis=-1, keepdims=True)
    var = jnp.mean(jnp.square(vg - mu), axis=-1, keepdims=True)
    vn = (vg - mu) * lax.rsqrt(var + EPS) * lng_ref[...] + lnb_ref[...]
    if maybe_vn_ref:
        maybe_vn_ref[0][...] = vn
    vnb = vn.astype(BF16)

    rows = x.shape[0]
    gw = width // n_sgu_groups
    mixed_chunks = []
    for c in range(rows // chunk):
        parts = [jnp.dot(mixw_ref[g], vnb[c * chunk:(c + 1) * chunk, g * gw:(g + 1) * gw],
                         preferred_element_type=F32) for g in range(n_sgu_groups)]
        mixed_chunks.append(jnp.concatenate(parts, axis=1) + mixb_ref[...])
    mixed = jnp.concatenate(mixed_chunks, axis=0) if len(mixed_chunks) > 1 else mixed_chunks[0]
    o_sgu = (u * mixed).astype(BF16)

    a_br = jnp.dot(oatt_ref[...].astype(BF16), wa_ref[...], preferred_element_type=F32)
    s_br = jnp.dot(o_sgu, ws_ref[...], preferred_element_type=F32)
    d = x.shape[1]
    m = (jax.nn.sigmoid(z[:, 2 * width:2 * width + d]) * a_br
         + jax.nn.sigmoid(z[:, 2 * width + d:]) * s_br)
    x1 = x + jnp.dot(m.astype(BF16), wo_ref[...], preferred_element_type=F32)
    x1_ref[...] = x1
    h2 = _rms(x1, g2_ref[...])
    h2_ref[...] = h2

    logits = jnp.dot(h2, wr_ref[...], preferred_element_type=F32,
                     precision=lax.Precision.HIGHEST) + br_ref[...]
    lane = lax.broadcasted_iota(jnp.int32, logits.shape, 1).astype(F32)
    big = float(LANES)
    is_group = lane < n_groups
    gl = jnp.where(is_group, logits, -jnp.inf)
    gmax = jnp.max(gl, axis=1, keepdims=True)
    g_sel = jnp.min(jnp.where(gl == gmax, lane, big), axis=1, keepdims=True)
    p_g = 1.0 / jnp.sum(jnp.where(is_group, jnp.exp(logits - gmax), 0.0), axis=1, keepdims=True)
    lo = n_groups + g_sel * experts_per_group
    el = jnp.where((lane >= lo) & (lane < lo + experts_per_group), logits, -jnp.inf)
    v1 = jnp.max(el, axis=1, keepdims=True)
    i1 = jnp.min(jnp.where(el == v1, lane, big), axis=1, keepdims=True)
    el2 = jnp.where(lane == i1, -jnp.inf, el)
    v2 = jnp.max(el2, axis=1, keepdims=True)
    i2 = jnp.min(jnp.where(el2 == v2, lane, big), axis=1, keepdims=True)
    t = jnp.exp(v2 - v1)
    w1 = p_g / (1.0 + t)
    w2 = p_g * t / (1.0 + t)
    route_ref[...] = jnp.where(lane == 0, i1 - n_groups,
                     jnp.where(lane == 1, i2 - n_groups,
                     jnp.where(lane == 2, w1,
                     jnp.where(lane == 3, w2, 0.0))))


def _mix_merge(x, o_att, weights, mix_w, mix_b, *, rows, chunk, n_sgu_groups, n_groups,
               experts_per_group, emit_vn, name):
    t, d = x.shape
    width = weights["ln_g"].shape[1]
    row_spec = lambda w: pl.BlockSpec((rows, w), lambda i: (i, 0))
    full = lambda a: pl.BlockSpec(a.shape, lambda i: (0,) * a.ndim)
    names = ["g1", "w2", "ln_g", "ln_b"]
    tail = ["wa", "ws", "wo", "g2", "wr", "br"]
    ins = [x, o_att] + [weights[n] for n in names] + [mix_w, mix_b] + [weights[n] for n in tail]
    in_specs = [row_spec(d), row_spec(o_att.shape[1])] + [full(a) for a in ins[2:]]
    out_specs = [row_spec(d), row_spec(d), row_spec(LANES)]
    out_shape = [jax.ShapeDtypeStruct((t, d), F32), jax.ShapeDtypeStruct((t, d), F32),
                 jax.ShapeDtypeStruct((t, LANES), F32)]
    if emit_vn:
        out_specs.append(row_spec(width))
        out_shape.append(jax.ShapeDtypeStruct((t, width), F32))
    kern = functools.partial(_mix_kernel, width=width, n_sgu_groups=n_sgu_groups, chunk=chunk,
                             n_groups=n_groups, experts_per_group=experts_per_group)
    return pl.pallas_call(
        kern, grid=(t // rows,), in_specs=in_specs, out_specs=out_specs, out_shape=out_shape,
        compiler_params=_params(1), name=name,
    )(*ins)


def _expert_kernel(blk_e_ref, nvalid_ref, tok_ref, dst_ref, h_hbm, wgu_ref, wd_ref, y_hbm,
                   xbuf, ybuf, gsem, ssem, *, rows, d_expert):
    del blk_e_ref
    nv = nvalid_ref[pl.program_id(0)]

    def gather(r):
        return pltpu.make_async_copy(h_hbm.at[pl.ds(tok_ref[0, 0, r], 1)],
                                     xbuf.at[pl.ds(r, 1)], gsem)

    def scatter(r):
        return pltpu.make_async_copy(ybuf.at[pl.ds(r, 1)],
                                     y_hbm.at[pl.ds(dst_ref[0, 0, r], 1)], ssem)

    def for_rows(n, fn):
        def body(r, c):
            fn(r)
            return c
        lax.fori_loop(0, n, body, 0)

    @pl.when(nv > 0)
    def _():
        for_rows(rows, lambda r: gather(r).start())
        for_rows(rows, lambda r: gather(r).wait())
        xb = xbuf[...].astype(BF16)
        gu = jnp.dot(xb, wgu_ref[0], preferred_element_type=F32)
        gate, up = gu[:, :d_expert], gu[:, d_expert:]
        act = (gate * jax.nn.sigmoid(gate)) * up
        ybuf[...] = jnp.dot(act.astype(BF16), wd_ref[0], preferred_element_type=F32)
        for_rows(nv, lambda r: scatter(r).start())
        for_rows(nv, lambda r: scatter(r).wait())


def _expert_mlp(h2, blk_e, nvalid, tok, dst, w_gu, w_d, *, n_blocks, rows, name):
    t, d = h2.shape
    d_expert = w_d.shape[1]
    idx_spec = pl.BlockSpec((1, 1, rows), lambda j, be, nv: (j, 0, 0), memory_space=pltpu.SMEM)
    kern = functools.partial(_expert_kernel, rows=rows, d_expert=d_expert)
    return pl.pallas_call(
        kern,
        grid_spec=pltpu.PrefetchScalarGridSpec(
            num_scalar_prefetch=2,
            grid=(n_blocks,),
            in_specs=[idx_spec, idx_spec,
                      pl.BlockSpec(memory_space=pl.ANY),
                      pl.BlockSpec((1, d, 2 * d_expert), lambda j, be, nv: (be[j], 0, 0)),
                      pl.BlockSpec((1, d_expert, d), lambda j, be, nv: (be[j], 0, 0))],
            out_specs=pl.BlockSpec(memory_space=pl.ANY),
            scratch_shapes=[pltpu.VMEM((rows, d), F32), pltpu.VMEM((rows, d), F32),
                            pltpu.SemaphoreType.DMA, pltpu.SemaphoreType.DMA]),
        out_shape=jax.ShapeDtypeStruct((TOP_K * t, d), F32),
        compiler_params=_params(1),
        name=name,
    )(blk_e, nvalid, tok, dst, h2, w_gu, w_d)


def _dispatch_plan(route, n_experts, rows):
    t = route.shape[0]
    tk = TOP_K * t
    flat_e = route[:, :TOP_K].astype(jnp.int32).reshape(tk)
    order = jnp.argsort(flat_e, stable=True).astype(jnp.int32)
    counts = jnp.sum((flat_e[:, None] == jnp.arange(n_experts, dtype=jnp.int32)[None, :])
                     .astype(jnp.int32), axis=0)
    blocks_e = (counts + rows - 1) // rows
    blk_end = jnp.cumsum(blocks_e)
    blk_start = blk_end - blocks_e
    start = jnp.cumsum(counts) - counts
    n_blocks = (tk + n_experts * (rows - 1) + rows - 1) // rows
    j = jnp.arange(n_blocks, dtype=jnp.int32)
    blk_e = jnp.minimum(jnp.searchsorted(blk_end, j, side="right"), n_experts - 1).astype(jnp.int32)
    within = j - blk_start[blk_e]
    nvalid = jnp.clip(counts[blk_e] - within * rows, 0, rows).astype(jnp.int32)
    r = jnp.arange(rows, dtype=jnp.int32)
    src = jnp.clip(start[blk_e][:, None] + within[:, None] * rows + r[None, :], 0, tk - 1)
    flat = jnp.where(r[None, :] < nvalid[:, None], order[src], 0)
    tok = (flat // TOP_K).reshape(n_blocks, 1, rows)
    dst = flat.reshape(n_blocks, 1, rows)
    return blk_e, nvalid, tok, dst, n_blocks


def _combine_kernel(x1_ref, y_ref, route_ref, *rest, d):
    o_ref = rest[-1]
    route = route_ref[...]
    y = y_ref[...]
    out = x1_ref[...] + (y[:, :d] * route[:, 2:3] + y[:, d:] * route[:, 3:4])
    if len(rest) == 2:
        out = _rms(out, rest[0][...])
    o_ref[...] = out


def _combine(x1, y_pairs, route, final_g, *, rows, name):
    t, d = x1.shape
    row_spec = lambda w: pl.BlockSpec((rows, w), lambda i: (i, 0))
    gains = [] if final_g is None else [final_g]
    return pl.pallas_call(
        functools.partial(_combine_kernel, d=d),
        grid=(t // rows,),
        in_specs=[row_spec(d), row_spec(TOP_K * d), row_spec(LANES)]
                 + [pl.BlockSpec(g.shape, lambda i: (0, 0)) for g in gains],
        out_specs=row_spec(d),
        out_shape=jax.ShapeDtypeStruct((t, d), F32),
        compiler_params=_params(1),
        name=name,
    )(x1, y_pairs, route, *gains)


def _row_tile(t, want):
    return want if t % want == 0 else t


def kernel(x_prompt, x_sample, cache_k, cache_v, page_table, norm1_g, w_in, lambda_q1, lambda_k1, lambda_q2, lambda_k2, subln_g, sgu_ln_g, sgu_ln_b, sgu_w, sgu_b, w_att_branch, w_sgu_branch, w_out, norm2_g, w_router_group, b_router_group, w_router_expert, b_router_expert, w_expert_gate, w_expert_up, w_expert_down, final_g):
    batch, seq, d = x_prompt.shape
    dec_batch, ds, _ = x_sample.shape
    depth, n_phys, page, n_maps, head_dim = cache_k.shape
    n_heads = n_maps // 2
    v_dim = 2 * head_dim
    qk_width = n_maps * head_dim
    att_width = n_heads * v_dim
    width = sgu_ln_g.shape[-1]
    n_sgu_groups, chunk = sgu_w.shape[1], sgu_w.shape[2]
    n_groups = w_router_group.shape[-1]
    experts_per_group = w_router_expert.shape[-1]
    n_experts = n_groups * experts_per_group
    n_pages = page_table.shape[1]
    past = n_pages * page
    assert 2 * qk_width + att_width + 2 * width + 2 * d == w_in.shape[-1]
    assert v_dim == LANES and seq % chunk == 0 and ds <= chunk and ds <= page
    assert n_groups + n_experts <= LANES

    tp, ts = batch * seq, dec_batch * ds
    xp = x_prompt.reshape(tp, d)
    xs = x_sample.reshape(ts, d)
    tab_p = _rope_tables(jnp.arange(seq, dtype=F32), head_dim)
    tab_s = _rope_tables(jnp.arange(ds, dtype=F32) + float(past), head_dim)
    tab_s = tuple(jnp.tile(a, (dec_batch, 1)) for a in tab_s)
    ck = cache_k.reshape(depth * n_phys, page, qk_width)
    cv = cache_v.reshape(depth * n_phys, page, att_width)
    row2 = lambda a: a.reshape(1, -1)

    proj_rows_p = _row_tile(seq, PROJ_ROWS)
    mix_rows_p = _row_tile(seq, MIX_ROWS)
    reps = ts // ds
    eye = jnp.eye(reps, dtype=F32)

    kp_l, vp_l, ks_l, vs_l, gs_l = [], [], [], [], []
    for layer in range(depth):
        lam_init = 0.8 - 0.6 * math.exp(-0.3 * layer)
        w_l = w_in[layer].astype(BF16)
        w_qkv = w_l[:, :2 * qk_width + att_width]
        lam_vecs = [row2(a[layer]) for a in (lambda_q1, lambda_k1, lambda_q2, lambda_k2)]
        g_sub = row2(subln_g[layer])
        tril = jnp.tril(sgu_w[layer])
        w_route = jnp.concatenate(
            [w_router_group[layer],
             jnp.transpose(w_router_expert[layer], (1, 0, 2)).reshape(d, n_experts)], axis=1)
        b_route = jnp.concatenate([b_router_group[layer], b_router_expert[layer].reshape(-1)])
        lane_pad = LANES - (n_groups + n_experts)
        weights = dict(
            g1=row2(norm1_g[layer]), w2=w_l[:, 2 * qk_width + att_width:],
            ln_g=row2(sgu_ln_g[layer]), ln_b=row2(sgu_ln_b[layer]),
            wa=w_att_branch[layer].astype(BF16), ws=w_sgu_branch[layer].astype(BF16),
            wo=w_out[layer].astype(BF16), g2=row2(norm2_g[layer]),
            wr=jnp.pad(w_route, ((0, 0), (0, lane_pad))),
            br=row2(jnp.pad(b_route, (0, lane_pad))))
        gw = width // n_sgu_groups
        mix_w_p = tril.astype(BF16)
        mix_b_p = jnp.repeat(sgu_b[layer].T, gw, axis=1)
        mix_w_s = jnp.einsum("ab,gts->gatbs", eye, tril[:, :ds, :ds]).reshape(
            n_sgu_groups, ts, ts).astype(BF16)
        mix_b_s = jnp.tile(jnp.repeat(sgu_b[layer][:, :ds].T, gw, axis=1), (reps, 1))
        w_gu = jnp.concatenate([w_expert_gate[layer], w_expert_up[layer]], axis=-1).astype(BF16)
        w_dn = w_expert_down[layer].astype(BF16)

        k_p, v_p, qb, kb, vb = _qkv_proj(
            xp, weights["g1"], w_qkv, tab_p, rows=proj_rows_p, tiles_per_seq=seq // proj_rows_p,
            qk_width=qk_width, att_width=att_width, head_dim=head_dim, q_dtype=BF16,
            name="qkv_prompt")
        o_att_p = _prompt_attention(qb, kb, vb, lam_vecs, g_sub, batch=batch, seq=seq,
                                    n_heads=n_heads, head_dim=head_dim, lam_init=lam_init,
                                    name="attn_prompt")
        x1_p, h2_p, route_p = _mix_merge(
            xp, o_att_p, weights, mix_w_p, mix_b_p, rows=mix_rows_p, chunk=chunk,
            n_sgu_groups=n_sgu_groups, n_groups=n_groups, experts_per_group=experts_per_group,
            emit_vn=False, name="mix_prompt")

        k_s, v_s, q_s, _, _ = _qkv_proj(
            xs, weights["g1"], w_qkv, tab_s, rows=ts, tiles_per_seq=1, qk_width=qk_width,
            att_width=att_width, head_dim=head_dim, q_dtype=F32, name="qkv_sample")
        page_ids = (page_table.astype(jnp.int32) + layer * n_phys).reshape(-1)
        o_att_s = _decode_attention(q_s, k_s, v_s, ck, cv, page_ids, lam_vecs, g_sub,
                                    dec_batch=dec_batch, ds=ds, n_pages=n_pages, n_maps=n_maps,
                                    head_dim=head_dim, lam_init=lam_init, name="attn_sample")
        x1_s, h2_s, route_s, vn_s = _mix_merge(
            xs, o_att_s, weights, mix_w_s, mix_b_s, rows=ts, chunk=ts,
            n_sgu_groups=n_sgu_groups, n_groups=n_groups, experts_per_group=experts_per_group,
            emit_vn=True, name="mix_sample")

        g_fin = row2(final_g) if layer == depth - 1 else None
        outs = []
        for tag, x1, h2, route in (("prompt", x1_p, h2_p, route_p),
                                   ("sample", x1_s, h2_s, route_s)):
            t = x1.shape[0]
            blk_e, nvalid, tok, dst, n_blocks = _dispatch_plan(route, n_experts, MOE_ROWS)
            y = _expert_mlp(h2, blk_e, nvalid, tok, dst, w_gu, w_dn, n_blocks=n_blocks,
                            rows=MOE_ROWS, name="experts_" + tag)
            outs.append(_combine(x1, y.reshape(t, TOP_K * d), route, g_fin,
                                 rows=_row_tile(t, OUT_ROWS), name="combine_" + tag))
        xp, xs = outs

        kp_l.append(k_p.reshape(batch, seq, n_maps, head_dim))
        vp_l.append(v_p.reshape(batch, seq, n_heads, v_dim))
        ks_l.append(k_s.reshape(dec_batch, ds, n_maps, head_dim))
        vs_l.append(v_s.reshape(dec_batch, ds, n_heads, v_dim))
        gs_l.append(vn_s.reshape(dec_batch, ds, width))

    return (xp.reshape(batch, seq, d), xs.reshape(dec_batch, ds, d),
            jnp.stack(kp_l), jnp.stack(vp_l), jnp.stack(ks_l), jnp.stack(vs_l),
            jnp.stack(gs_l))
```

```python
import functools
import math

import jax
import jax.numpy as jnp
from jax import lax
from jax.experimental import pallas as pl
from jax.experimental.pallas import tpu as pltpu

F32 = jnp.float32
BF16 = jnp.bfloat16

EPS = 1e-6
ROPE_THETA = 10000.0
TOP_K = 2
LOG2E = math.log2(math.e)
LANES = 128
SUBLANES = 8
VMEM_LIMIT_BYTES = 56 * 1024 * 1024

PROJ_ROWS = 512
MIX_ROWS = 256
ATT_BLOCK = 1024
ATT_ROW_CHUNK = 256
PAGES_PER_STEP = 8
MOE_ROWS = 256
OUT_ROWS = 512


def _params(n_axes):
    return pltpu.CompilerParams(dimension_semantics=("arbitrary",) * n_axes,
                                vmem_limit_bytes=VMEM_LIMIT_BYTES)


def _rms(x, g):
    return (x * lax.rsqrt(jnp.mean(x * x, axis=-1, keepdims=True) + EPS)) * g


def _log2(n):
    assert n > 0 and n & (n - 1) == 0, n
    return n.bit_length() - 1


def _shr(x, n):
    return lax.shift_right_logical(x, _log2(n))


def _rope_tables(pos, head_dim):
    inv = 1.0 / (ROPE_THETA ** (jnp.arange(0, head_dim, 2, dtype=F32) / head_dim))
    ang = pos[:, None] * inv[None, :]
    cos, sin = jnp.cos(ang), jnp.sin(ang)
    zero = jnp.zeros_like(sin)
    reps = LANES // head_dim
    cos_t = jnp.concatenate([cos, cos] * reps, axis=-1)
    sin_lo = jnp.concatenate([-sin, zero] * reps, axis=-1)
    sin_hi = jnp.concatenate([zero, sin] * reps, axis=-1)
    return cos_t, sin_lo, sin_hi


def _qkv_kernel(x_ref, g_ref, w_ref, cos_ref, slo_ref, shi_ref,
                kout_ref, vout_ref, q_ref, kb_ref, vb_ref,
                *, qk_width, head_dim, n_maps, n_heads, q_scale):
    rows = x_ref.shape[0]
    half = head_dim // 2
    h = _rms(x_ref[...], g_ref[...])
    z = jnp.dot(h.astype(BF16), w_ref[...], preferred_element_type=F32)
    cos, slo, shi = cos_ref[...], slo_ref[...], shi_ref[...]

    def rope(blk):
        return (blk * cos + pltpu.roll(blk, LANES - half, 1) * slo
                + pltpu.roll(blk, half, 1) * shi)

    maps_per_block = LANES // head_dim
    for j in range(qk_width // LANES):
        cols = slice(j * LANES, (j + 1) * LANES)
        q_ref[:, cols] = (rope(z[:, cols]) * q_scale).astype(q_ref.dtype)
        kr = rope(z[:, qk_width + j * LANES: qk_width + (j + 1) * LANES])
        kb_ref[:, cols] = kr.astype(BF16)
        for i in range(maps_per_block):
            part = kr if i == 0 else pltpu.roll(kr, LANES - i * head_dim, 1)
            kout_ref[pl.ds(j * maps_per_block + i, rows, stride=n_maps), :] = part[:, :head_dim]
    v = z[:, 2 * qk_width:]
    vb_ref[...] = v.astype(BF16)
    for hd in range(n_heads):
        vout_ref[pl.ds(hd, rows, stride=n_heads), :] = v[:, hd * LANES:(hd + 1) * LANES]


def _qkv_proj(x, g, w_qkv, tables, *, rows, tiles_per_seq, n_maps, head_dim, q_dtype, name):
    t, d = x.shape
    n_heads = n_maps // 2
    qk_width = n_maps * head_dim
    att_width = n_heads * 2 * head_dim
    row_spec = lambda w: pl.BlockSpec((rows, w), lambda i: (i, 0))
    tab_spec = pl.BlockSpec((rows, LANES), lambda i: (i % tiles_per_seq, 0))
    full = lambda a: pl.BlockSpec(a.shape, lambda i: (0,) * a.ndim)
    kern = functools.partial(_qkv_kernel, qk_width=qk_width, head_dim=head_dim, n_maps=n_maps,
                             n_heads=n_heads, q_scale=head_dim ** -0.5 * LOG2E)
    return pl.pallas_call(
        kern,
        grid=(t // rows,),
        in_specs=[row_spec(d), full(g), full(w_qkv), tab_spec, tab_spec, tab_spec],
        out_specs=[pl.BlockSpec((rows * n_maps, head_dim), lambda i: (i, 0)),
                   pl.BlockSpec((rows * n_heads, 2 * head_dim), lambda i: (i, 0)),
                   row_spec(qk_width), row_spec(qk_width), row_spec(att_width)],
        out_shape=[jax.ShapeDtypeStruct((t * n_maps, head_dim), F32),
                   jax.ShapeDtypeStruct((t * n_heads, 2 * head_dim), F32),
                   jax.ShapeDtypeStruct((t, qk_width), q_dtype),
                   jax.ShapeDtypeStruct((t, qk_width), BF16),
                   jax.ShapeDtypeStruct((t, att_width), BF16)],
        compiler_params=_params(1),
        name=name,
    )(x, g, w_qkv, *tables)


def _diff_lambda(lq1_ref, lk1_ref, lq2_ref, lk2_ref, lam_init):
    s1 = jnp.sum(lq1_ref[...] * lk1_ref[...], axis=-1, keepdims=True)
    s2 = jnp.sum(lq2_ref[...] * lk2_ref[...], axis=-1, keepdims=True)
    return jnp.exp(s1) - jnp.exp(s2) + lam_init


def _subln(o1, o2, lam, g, lam_init):
    d = o1 - lam * o2
    return _rms(d, g) * (1.0 - lam_init)


def _prompt_attn_kernel(qi_ref, ki_ref, q_ref, k_ref, v_ref, lq1_ref, lk1_ref, lq2_ref,
                        lk2_ref, g_ref, o_ref, q2_ref, m_ref, acc_ref,
                        *, blk, chunk, head_dim, lam_init):
    p = pl.program_id(2)
    qi, ki = qi_ref[p], ki_ref[p]
    v_dim = 2 * head_dim

    @pl.when(ki == 0)
    def _init():
        q = q_ref[...]
        lane = lax.broadcasted_iota(jnp.int32, q.shape, 1)
        zero = jnp.zeros_like(q)
        q2_ref[:blk, :] = jnp.where(lane < head_dim, q, zero)
        q2_ref[blk:, :] = jnp.where(lane >= head_dim, q, zero)
        m_ref[...] = jnp.full(m_ref.shape, -jnp.inf, F32)
        acc_ref[...] = jnp.zeros(acc_ref.shape, F32)

    def step(masked):
        for c in range(2 * blk // chunk):
            rows = slice(c * chunk, (c + 1) * chunk)
            first = (c * chunk) % blk
            n_keys = first + chunk if masked else blk
            v1 = jnp.concatenate([v_ref[:n_keys, :], jnp.ones((n_keys, LANES), BF16)], axis=1)
            s = lax.dot_general(q2_ref[rows, :], k_ref[:n_keys, :], (((1,), (1,)), ((), ())),
                                preferred_element_type=F32)
            if masked:
                r = lax.broadcasted_iota(jnp.int32, s.shape, 0) + first
                col = lax.broadcasted_iota(jnp.int32, s.shape, 1)
                s = jnp.where(col <= r, s, -jnp.inf)
            m_prev = m_ref[rows, :]
            m_new = jnp.maximum(m_prev, jnp.max(s, axis=1, keepdims=True))
            alpha = jnp.exp2(m_prev - m_new)
            pexp = jnp.exp2(s - jnp.concatenate([m_new] * (n_keys // LANES), axis=1))
            acc_ref[rows, :] = (jnp.concatenate([alpha] * (acc_ref.shape[1] // LANES), axis=1)
                                * acc_ref[rows, :]
                                + jnp.dot(pexp.astype(BF16), v1, preferred_element_type=F32))
            m_ref[rows, :] = m_new

    @pl.when(ki < qi)
    def _off_diagonal():
        step(False)

    @pl.when(ki == qi)
    def _diagonal():
        step(True)
        acc = acc_ref[...]
        o = acc[:, :v_dim] / acc[:, v_dim:]
        lam = _diff_lambda(lq1_ref, lk1_ref, lq2_ref, lk2_ref, lam_init)
        o_ref[...] = _subln(o[:blk], o[blk:], lam, g_ref[...], lam_init).astype(o_ref.dtype)


def _prompt_attention(qb, kb, vb, lam_vecs, subln_g, *, batch, seq, n_heads, head_dim,
                      lam_init, name):
    blk = min(ATT_BLOCK, seq)
    chunk = min(ATT_ROW_CHUNK, blk)
    nq = seq // blk
    pairs = [(i, j) for i in range(nq) for j in range(i + 1)]
    qi_map = jnp.asarray([a for a, _ in pairs], jnp.int32)
    ki_map = jnp.asarray([b for _, b in pairs], jnp.int32)
    v_dim = 2 * head_dim
    assert v_dim == LANES
    q_spec = pl.BlockSpec((blk, v_dim), lambda b, h, p, qi, ki: (b * nq + qi[p], h))
    kv_spec = pl.BlockSpec((blk, v_dim), lambda b, h, p, qi, ki: (b * nq + ki[p], h))
    vec = lambda a: pl.BlockSpec(a.shape, lambda b, h, p, qi, ki: (0, 0))
    kern = functools.partial(_prompt_attn_kernel, blk=blk, chunk=chunk, head_dim=head_dim,
                             lam_init=lam_init)
    return pl.pallas_call(
        kern,
        grid_spec=pltpu.PrefetchScalarGridSpec(
            num_scalar_prefetch=2,
            grid=(batch, n_heads, len(pairs)),
            in_specs=[q_spec, kv_spec, kv_spec] + [vec(a) for a in lam_vecs] + [vec(subln_g)],
            out_specs=q_spec,
            scratch_shapes=[pltpu.VMEM((2 * blk, v_dim), BF16),
                            pltpu.VMEM((2 * blk, LANES), F32),
                            pltpu.VMEM((2 * blk, v_dim + LANES), F32)]),
        out_shape=jax.ShapeDtypeStruct((batch * seq, n_heads * v_dim), BF16),
        compiler_params=_params(3),
        name=name,
    )(qi_map, ki_map, qb, kb, vb, *lam_vecs, subln_g)


def _decode_attn_kernel(*refs, n_pages_step, page, ds, n_maps, head_dim, lam_init):
    pps = n_pages_step
    w_ref, knew_ref, vnew_ref = refs[1:4]
    k_refs = refs[4:4 + pps]
    v_refs = refs[4 + pps:4 + 2 * pps]
    lq1_ref, lk1_ref, lq2_ref, lk2_ref, g_ref = refs[4 + 2 * pps:9 + 2 * pps]
    o_ref = refs[9 + 2 * pps]
    m_ref, l_ref, acc_ref = refs[10 + 2 * pps:]
    j = pl.program_id(1)
    n_heads = n_maps // 2
    v_dim = 2 * head_dim
    rows = page * n_heads

    lane = lax.broadcasted_iota(jnp.int32, (rows, LANES), 1)
    row = lax.broadcasted_iota(jnp.int32, (rows, LANES), 0)
    own_head = _shr(lane, 2 * ds) == (row & (n_heads - 1))
    second = (_shr(lane, ds) & 1) == 1
    w = w_ref[0]

    def page_update(k_ref, v_ref, causal=None):
        parts = []
        for par in range(2):
            kj = k_ref[0, :, pl.ds(par, n_heads, stride=2), :]
            parts.append(jnp.dot(kj.reshape(rows, head_dim).astype(BF16), w,
                                 preferred_element_type=F32))
        s = jnp.where(own_head, jnp.where(second, parts[1], parts[0]), -jnp.inf)
        if causal is not None:
            s = jnp.where(causal, s, -jnp.inf)
        m_prev = m_ref[...]
        m_new = jnp.maximum(m_prev, jnp.max(s, axis=0, keepdims=True))
        alpha = jnp.exp2(m_prev - m_new)
        pexp = jnp.exp2(s - m_new[0:1, :])
        l_ref[...] = alpha * l_ref[...] + jnp.sum(pexp, axis=0, keepdims=True)
        v2 = v_ref[0].reshape(rows, v_dim).astype(BF16)
        pv = lax.dot_general(v2, pexp.astype(BF16), (((0,), (0,)), ((), ())),
                             preferred_element_type=F32)
        acc_ref[...] = jnp.concatenate([alpha] * (v_dim // SUBLANES), axis=0) * acc_ref[...] + pv
        m_ref[...] = m_new

    @pl.when(j == 0)
    def _init():
        m_ref[...] = jnp.full(m_ref.shape, -jnp.inf, F32)
        l_ref[...] = jnp.zeros(l_ref.shape, F32)
        acc_ref[...] = jnp.zeros(acc_ref.shape, F32)
        page_update(knew_ref, vnew_ref, causal=_shr(row, n_heads) <= (lane & (ds - 1)))

    for i in range(pps):
        page_update(k_refs[i], v_refs[i])

    @pl.when(j == pl.num_programs(1) - 1)
    def _finish():
        lam = _diff_lambda(lq1_ref, lk1_ref, lq2_ref, lk2_ref, lam_init)
        inv = 1.0 / l_ref[...]
        o = (acc_ref[...] * jnp.concatenate([inv] * (v_dim // SUBLANES), axis=0)).T
        for h in range(n_heads):
            o1 = o[2 * h * ds:(2 * h + 1) * ds]
            o2 = o[(2 * h + 1) * ds:(2 * h + 2) * ds]
            o_ref[:, h * v_dim:(h + 1) * v_dim] = _subln(o1, o2, lam, g_ref[...],
                                                         lam_init).astype(o_ref.dtype)


def _decode_attention(q, k_new, v_new, cache_k, cache_v, page_ids, lam_vecs, subln_g, *,
                      dec_batch, ds, n_pages, lam_init, name):
    n_phys, page, n_maps, head_dim = cache_k.shape
    n_heads, v_dim = cache_v.shape[2:]
    assert n_maps * ds == LANES and v_dim == LANES
    pps = math.gcd(PAGES_PER_STEP, n_pages)
    n_steps = n_pages // pps
    w = q.reshape(dec_batch, ds, n_maps, head_dim).transpose(0, 3, 2, 1).reshape(
        dec_batch, head_dim, LANES).astype(BF16)
    pad = lambda a: jnp.pad(a.reshape(dec_batch, ds, *a.shape[1:]),
                            ((0, 0), (0, page - ds), (0, 0), (0, 0)))
    k_new = pad(k_new.reshape(dec_batch * ds, n_maps, head_dim))
    v_new = pad(v_new.reshape(dec_batch * ds, n_heads, v_dim))
    w_spec = pl.BlockSpec((1, head_dim, LANES), lambda b, j, pt: (b, 0, 0))
    knew_spec = pl.BlockSpec((1, page, n_maps, head_dim), lambda b, j, pt: (b, 0, 0, 0))
    vnew_spec = pl.BlockSpec((1, page, n_heads, v_dim), lambda b, j, pt: (b, 0, 0, 0))

    def page_spec(i, shape):
        return pl.BlockSpec((1,) + shape,
                            lambda b, j, pt: (pt[b * n_pages + j * pps + i], 0, 0, 0))

    vec = lambda a: pl.BlockSpec(a.shape, lambda b, j, pt: (0, 0))
    kern = functools.partial(_decode_attn_kernel, n_pages_step=pps, page=page, ds=ds,
                             n_maps=n_maps, head_dim=head_dim, lam_init=lam_init)
    return pl.pallas_call(
        kern,
        grid_spec=pltpu.PrefetchScalarGridSpec(
            num_scalar_prefetch=1,
            grid=(dec_batch, n_steps),
            in_specs=([w_spec, knew_spec, vnew_spec]
                      + [page_spec(i, (page, n_maps, head_dim)) for i in range(pps)]
                      + [page_spec(i, (page, n_heads, v_dim)) for i in range(pps)]
                      + [vec(a) for a in lam_vecs] + [vec(subln_g)]),
            out_specs=pl.BlockSpec((ds, n_heads * v_dim), lambda b, j, pt: (b, 0)),
            scratch_shapes=[pltpu.VMEM((SUBLANES, LANES), F32),
                            pltpu.VMEM((SUBLANES, LANES), F32),
                            pltpu.VMEM((v_dim, LANES), F32)]),
        out_shape=jax.ShapeDtypeStruct((dec_batch * ds, n_heads * v_dim), F32),
        compiler_params=_params(2),
        name=name,
    )(page_ids, w, k_new, v_new, *([cache_k] * pps), *([cache_v] * pps), *lam_vecs, subln_g)


def _gelu_tanh(x):
    return 0.5 * x * (1.0 + jnp.tanh(math.sqrt(2.0 / math.pi) * (x + 0.044715 * (x * x * x))))


def _mix_kernel(x_ref, oatt_ref, g1_ref, w2_ref, lng_ref, lnb_ref, mixw_ref, mixb_ref,
                wa_ref, ws_ref, wo_ref, g2_ref, wr_ref, br_ref,
                x1_ref, h2_ref, route_ref, *maybe_vn_ref,
                width, n_sgu_groups, chunk, n_groups, experts_per_group):
    x = x_ref[...]
    h = _rms(x, g1_ref[...]).astype(BF16)
    z = jnp.dot(h, w2_ref[...], preferred_element_type=F32)
    u = _gelu_tanh(z[:, :width])
    vg = _gelu_tanh(z[:, width:2 * width])
    mu = jnp.mean(vg, axis=-1, keepdims=True)
    var = jnp.mean(jnp.square(vg - mu), axis=-1, keepdims=True)
    vn = (vg - mu) * lax.rsqrt(var + EPS) * lng_ref[...] + lnb_ref[...]
    if maybe_vn_ref:
        maybe_vn_ref[0][...] = vn
    vnb = vn.astype(BF16)

    rows = x.shape[0]
    gw = width // n_sgu_groups
    mixed_chunks = []
    for c in range(rows // chunk):
        parts = [jnp.dot(mixw_ref[g], vnb[c * chunk:(c + 1) * chunk, g * gw:(g + 1) * gw],
                         preferred_element_type=F32) for g in range(n_sgu_groups)]
        mixed_chunks.append(jnp.concatenate(parts, axis=1) + mixb_ref[...])
    mixed = jnp.concatenate(mixed_chunks, axis=0) if len(mixed_chunks) > 1 else mixed_chunks[0]
    o_sgu = (u * mixed).astype(BF16)

    a_br = jnp.dot(oatt_ref[...].astype(BF16), wa_ref[...], preferred_element_type=F32)
    s_br = jnp.dot(o_sgu, ws_ref[...], preferred_element_type=F32)
    d = x.shape[1]
    m = (jax.nn.sigmoid(z[:, 2 * width:2 * width + d]) * a_br
         + jax.nn.sigmoid(z[:, 2 * width + d:]) * s_br)
    x1 = x + jnp.dot(m.astype(BF16), wo_ref[...], preferred_element_type=F32)
    x1_ref[...] = x1
    h2 = _rms(x1, g2_ref[...])
    h2_ref[...] = h2

    logits = jnp.dot(h2, wr_ref[...], preferred_element_type=F32,
                     precision=lax.Precision.HIGHEST) + br_ref[...]
    lane = lax.broadcasted_iota(jnp.int32, logits.shape, 1).astype(F32)
    big = float(LANES)
    is_group = lane < n_groups
    gl = jnp.where(is_group, logits, -jnp.inf)
    gmax = jnp.max(gl, axis=1, keepdims=True)
    g_sel = jnp.min(jnp.where(gl == gmax, lane, big), axis=1, keepdims=True)
    p_g = 1.0 / jnp.sum(jnp.where(is_group, jnp.exp(logits - gmax), 0.0), axis=1, keepdims=True)
    lo = n_groups + g_sel * experts_per_group
    el = jnp.where((lane >= lo) & (lane < lo + experts_per_group), logits, -jnp.inf)
    v1 = jnp.max(el, axis=1, keepdims=True)
    i1 = jnp.min(jnp.where(el == v1, lane, big), axis=1, keepdims=True)
    el2 = jnp.where(lane == i1, -jnp.inf, el)
    v2 = jnp.max(el2, axis=1, keepdims=True)
    i2 = jnp.min(jnp.where(el2 == v2, lane, big), axis=1, keepdims=True)
    t = jnp.exp(v2 - v1)
    w1 = p_g / (1.0 + t)
    w2 = p_g * t / (1.0 + t)
    route_ref[...] = jnp.where(lane == 0, i1 - n_groups,
                     jnp.where(lane == 1, i2 - n_groups,
                     jnp.where(lane == 2, w1,
                     jnp.where(lane == 3, w2, 0.0))))


def _mix_merge(x, o_att, weights, mix_w, mix_b, *, rows, chunk, n_sgu_groups, n_groups,
               experts_per_group, emit_vn, name):
    t, d = x.shape
    width = weights["ln_g"].shape[1]
    row_spec = lambda w: pl.BlockSpec((rows, w), lambda i: (i, 0))
    full = lambda a: pl.BlockSpec(a.shape, lambda i: (0,) * a.ndim)
    names = ["g1", "w2", "ln_g", "ln_b"]
    tail = ["wa", "ws", "wo", "g2", "wr", "br"]
    ins = [x, o_att] + [weights[n] for n in names] + [mix_w, mix_b] + [weights[n] for n in tail]
    in_specs = [row_spec(d), row_spec(o_att.shape[1])] + [full(a) for a in ins[2:]]
    out_specs = [row_spec(d), row_spec(d), row_spec(LANES)]
    out_shape = [jax.ShapeDtypeStruct((t, d), F32), jax.ShapeDtypeStruct((t, d), F32),
                 jax.ShapeDtypeStruct((t, LANES), F32)]
    if emit_vn:
        out_specs.append(row_spec(width))
        out_shape.append(jax.ShapeDtypeStruct((t, width), F32))
    kern = functools.partial(_mix_kernel, width=width, n_sgu_groups=n_sgu_groups, chunk=chunk,
                             n_groups=n_groups, experts_per_group=experts_per_group)
    return pl.pallas_call(
        kern, grid=(t // rows,), in_specs=in_specs, out_specs=out_specs, out_shape=out_shape,
        compiler_params=_params(1), name=name,
    )(*ins)


def _swiglu(xb, wgu, wd, d_expert):
    gu = jnp.dot(xb, wgu, preferred_element_type=F32)
    gate, up = gu[:, :d_expert], gu[:, d_expert:]
    act = (gate * jax.nn.sigmoid(gate)) * up
    return jnp.dot(act.astype(BF16), wd, preferred_element_type=F32)


def _expert_kernel(blk_e_ref, n_active_ref, tok_ref, tok_next_ref, dst_ref, h_hbm, wgu_ref,
                   wd_ref, y_hbm, xbuf, ybuf, gsem, ssem, *, rows, d_expert, n_blocks):
    del blk_e_ref
    j = pl.program_id(0)
    n_active = n_active_ref[0]
    slot = lax.rem(j, 2)

    def start_gather(ids_ref, s):
        for r in range(rows):
            pltpu.make_async_copy(h_hbm.at[pl.ds(ids_ref[0, 0, r], 1)],
                                  xbuf.at[s, pl.ds(r, 1)], gsem.at[s]).start()

    def wait_gather(s):
        pltpu.make_async_copy(h_hbm.at[pl.ds(0, rows)], xbuf.at[s], gsem.at[s]).wait()

    def wait_scatter(s):
        pltpu.make_async_copy(ybuf.at[s], y_hbm.at[pl.ds(0, rows)], ssem.at[s]).wait()

    @pl.when(j == 0)
    def _prime():
        start_gather(tok_ref, 0)
        ybuf[...] = jnp.zeros(ybuf.shape, F32)
        n_real = y_hbm.shape[0] - 2 * rows
        for s in range(2):
            spare = pltpu.make_async_copy(ybuf.at[s], y_hbm.at[pl.ds(n_real + s * rows, rows)],
                                          ssem.at[s])
            spare.start()
            spare.wait()

    @pl.when(j < n_active)
    def _block():
        wait_gather(slot)

        @pl.when(j >= 2)
        def _():
            wait_scatter(slot)

        start_gather(tok_next_ref, 1 - slot)
        ybuf[slot] = _swiglu(xbuf[slot].astype(BF16), wgu_ref[0], wd_ref[0], d_expert)
        for r in range(rows):
            pltpu.make_async_copy(ybuf.at[slot, pl.ds(r, 1)],
                                  y_hbm.at[pl.ds(dst_ref[0, 0, r], 1)], ssem.at[slot]).start()

    @pl.when(j == n_blocks - 1)
    def _drain():
        wait_gather(lax.rem(n_active, 2))

        @pl.when(n_active >= 1)
        def _():
            wait_scatter(lax.rem(n_active - 1, 2))

        @pl.when(n_active >= 2)
        def _():
            wait_scatter(lax.rem(n_active, 2))


def _expert_mlp(h2, plan, w_gu, w_d, *, rows, name):
    blk_e, n_active, tok, dst, n_blocks = plan
    t, d = h2.shape
    d_expert = w_d.shape[1]
    smem = lambda imap: pl.BlockSpec((1, 1, rows), imap, memory_space=pltpu.SMEM)
    kern = functools.partial(_expert_kernel, rows=rows, d_expert=d_expert, n_blocks=n_blocks)
    return pl.pallas_call(
        kern,
        grid_spec=pltpu.PrefetchScalarGridSpec(
            num_scalar_prefetch=2,
            grid=(n_blocks,),
            in_specs=[smem(lambda j, be, na: (j, 0, 0)),
                      smem(lambda j, be, na: (jnp.minimum(j + 1, n_blocks - 1), 0, 0)),
                      smem(lambda j, be, na: (j, 0, 0)),
                      pl.BlockSpec(memory_space=pl.ANY),
                      pl.BlockSpec((1, d, 2 * d_expert), lambda j, be, na: (be[j], 0, 0)),
                      pl.BlockSpec((1, d_expert, d), lambda j, be, na: (be[j], 0, 0))],
            out_specs=pl.BlockSpec(memory_space=pl.ANY),
            scratch_shapes=[pltpu.VMEM((2, rows, d), F32), pltpu.VMEM((2, rows, d), F32),
                            pltpu.SemaphoreType.DMA((2,)), pltpu.SemaphoreType.DMA((2,))]),
        out_shape=jax.ShapeDtypeStruct((TOP_K * (t + rows), d), F32),
        compiler_params=_params(1),
        name=name,
    )(blk_e, n_active, tok, tok, dst, h2, w_gu, w_d)


def _dispatch_plan(route, n_experts, rows):
    t = route.shape[0]
    tk = TOP_K * t
    flat_e = route[:, :TOP_K].astype(jnp.int32).reshape(tk)
    order = jnp.argsort(flat_e, stable=True).astype(jnp.int32)
    experts = jnp.arange(n_experts, dtype=jnp.int32)
    counts = jnp.sum((flat_e[:, None] == experts[None, :]).astype(jnp.int32), axis=0)
    blocks_e = (counts + rows - 1) // rows
    blk_end = jnp.cumsum(blocks_e)
    blk_start = blk_end - blocks_e
    start = jnp.cumsum(counts) - counts
    n_blocks = (tk + n_experts * (rows - 1) + rows - 1) // rows
    j = jnp.arange(n_blocks, dtype=jnp.int32)
    blk_e = jnp.minimum(jnp.sum((j[:, None] >= blk_end[None, :]).astype(jnp.int32), axis=1),
                        n_experts - 1)
    within = j - blk_start[blk_e]
    nvalid = jnp.clip(counts[blk_e] - within * rows, 0, rows)
    r = jnp.arange(rows, dtype=jnp.int32)
    valid = r[None, :] < nvalid[:, None]
    src = jnp.clip(start[blk_e][:, None] + within[:, None] * rows + r[None, :], 0, tk - 1)
    flat = order[src]
    tok = jnp.where(valid, flat // TOP_K, 0).reshape(n_blocks, 1, rows)
    spare = tk + (j & 1)[:, None] * rows + r[None, :]
    dst = jnp.where(valid, flat, spare).reshape(n_blocks, 1, rows)
    n_active = blk_end[-1:].astype(jnp.int32)
    return blk_e, n_active, tok, dst, n_blocks


def _combine_kernel(x1_ref, y_ref, route_ref, *rest, d):
    o_ref = rest[-1]
    route = route_ref[...]
    y = y_ref[...]
    out = x1_ref[...] + (y[:, :d] * route[:, 2:3] + y[:, d:] * route[:, 3:4])
    if len(rest) == 2:
        out = _rms(out, rest[0][...])
    o_ref[...] = out


def _combine(x1, y_pairs, route, final_g, *, rows, name):
    t, d = x1.shape
    row_spec = lambda w: pl.BlockSpec((rows, w), lambda i: (i, 0))
    gains = [] if final_g is None else [final_g]
    return pl.pallas_call(
        functools.partial(_combine_kernel, d=d),
        grid=(t // rows,),
        in_specs=[row_spec(d), row_spec(TOP_K * d), row_spec(LANES)]
                 + [pl.BlockSpec(g.shape, lambda i: (0, 0)) for g in gains],
        out_specs=row_spec(d),
        out_shape=jax.ShapeDtypeStruct((t, d), F32),
        compiler_params=_params(1),
        name=name,
    )(x1, y_pairs, route, *gains)


def _dense_moe_kernel(x1_ref, h2_ref, route_ref, wgu_ref, wd_ref, *rest, d_expert):
    o_ref, acc_ref = rest[-2:]
    e = pl.program_id(0)

    @pl.when(e == 0)
    def _():
        acc_ref[...] = jnp.zeros(acc_ref.shape, F32)

    route = route_ref[...]
    ef = e.astype(F32)
    coef = (jnp.where(route[:, 0:1] == ef, route[:, 2:3], 0.0)
            + jnp.where(route[:, 1:2] == ef, route[:, 3:4], 0.0))
    routed = (route[:, 0:1] == ef) | (route[:, 1:2] == ef)
    y = _swiglu(h2_ref[...].astype(BF16), wgu_ref[0], wd_ref[0], d_expert)
    acc_ref[...] += jnp.where(routed, y * coef, 0.0)

    @pl.when(e == pl.num_programs(0) - 1)
    def _():
        out = x1_ref[...] + acc_ref[...]
        if len(rest) == 3:
            out = _rms(out, rest[0][...])
        o_ref[...] = out


def _dense_moe(x1, h2, route, w_gu, w_d, final_g, *, name):
    t, d = x1.shape
    n_experts, d_expert = w_d.shape[:2]
    full = lambda a: pl.BlockSpec(a.shape, lambda e: (0,) * a.ndim)
    gains = [] if final_g is None else [final_g]
    return pl.pallas_call(
        functools.partial(_dense_moe_kernel, d_expert=d_expert),
        grid=(n_experts,),
        in_specs=[full(x1), full(h2), full(route),
                  pl.BlockSpec((1, d, 2 * d_expert), lambda e: (e, 0, 0)),
                  pl.BlockSpec((1, d_expert, d), lambda e: (e, 0, 0))] + [full(g) for g in gains],
        out_specs=full(x1),
        out_shape=jax.ShapeDtypeStruct((t, d), F32),
        scratch_shapes=[pltpu.VMEM((t, d), F32)],
        compiler_params=_params(1),
        name=name,
    )(x1, h2, route, w_gu, w_d, *gains)


def _row_tile(t, want):
    return want if t % want == 0 else t


def kernel(x_prompt, x_sample, cache_k, cache_v, page_table, norm1_g, w_in, lambda_q1, lambda_k1, lambda_q2, lambda_k2, subln_g, sgu_ln_g, sgu_ln_b, sgu_w, sgu_b, w_att_branch, w_sgu_branch, w_out, norm2_g, w_router_group, b_router_group, w_router_expert, b_router_expert, w_expert_gate, w_expert_up, w_expert_down, final_g):
    batch, seq, d = x_prompt.shape
    dec_batch, ds, _ = x_sample.shape
    depth, n_phys, page, n_maps, head_dim = cache_k.shape
    n_heads = n_maps // 2
    v_dim = 2 * head_dim
    qk_width = n_maps * head_dim
    att_width = n_heads * v_dim
    width = sgu_ln_g.shape[-1]
    n_sgu_groups, chunk = sgu_w.shape[1], sgu_w.shape[2]
    n_groups = w_router_group.shape[-1]
    experts_per_group = w_router_expert.shape[-1]
    n_experts = n_groups * experts_per_group
    n_pages = page_table.shape[1]
    past = n_pages * page
    assert 2 * qk_width + att_width + 2 * width + 2 * d == w_in.shape[-1]
    assert v_dim == LANES and seq % chunk == 0 and ds <= chunk and ds <= page
    assert n_groups + n_experts <= LANES

    tp, ts = batch * seq, dec_batch * ds
    xp = x_prompt.reshape(tp, d)
    xs = x_sample.reshape(ts, d)
    tab_p = _rope_tables(jnp.arange(seq, dtype=F32), head_dim)
    tab_s = _rope_tables(jnp.arange(ds, dtype=F32) + float(past), head_dim)
    tab_s = tuple(jnp.tile(a, (dec_batch, 1)) for a in tab_s)
    ck = cache_k.reshape(depth * n_phys, page, n_maps, head_dim)
    cv = cache_v.reshape(depth * n_phys, page, n_heads, v_dim)
    row2 = lambda a: a.reshape(1, -1)

    proj_rows_p = _row_tile(seq, PROJ_ROWS)
    mix_rows_p = _row_tile(seq, MIX_ROWS)
    reps = ts // ds
    eye = jnp.eye(reps, dtype=F32)

    kp_l, vp_l, ks_l, vs_l, gs_l = [], [], [], [], []
    for layer in range(depth):
        lam_init = 0.8 - 0.6 * math.exp(-0.3 * layer)
        w_l = w_in[layer].astype(BF16)
        w_qkv = w_l[:, :2 * qk_width + att_width]
        lam_vecs = [row2(a[layer]) for a in (lambda_q1, lambda_k1, lambda_q2, lambda_k2)]
        g_sub = row2(subln_g[layer])
        tril = jnp.tril(sgu_w[layer])
        w_route = jnp.concatenate(
            [w_router_group[layer],
             jnp.transpose(w_router_expert[layer], (1, 0, 2)).reshape(d, n_experts)], axis=1)
        b_route = jnp.concatenate([b_router_group[layer], b_router_expert[layer].reshape(-1)])
        lane_pad = LANES - (n_groups + n_experts)
        weights = dict(
            g1=row2(norm1_g[layer]), w2=w_l[:, 2 * qk_width + att_width:],
            ln_g=row2(sgu_ln_g[layer]), ln_b=row2(sgu_ln_b[layer]),
            wa=w_att_branch[layer].astype(BF16), ws=w_sgu_branch[layer].astype(BF16),
            wo=w_out[layer].astype(BF16), g2=row2(norm2_g[layer]),
            wr=jnp.pad(w_route, ((0, 0), (0, lane_pad))),
            br=row2(jnp.pad(b_route, (0, lane_pad))))
        gw = width // n_sgu_groups
        mix_w_p = tril.astype(BF16)
        mix_b_p = jnp.repeat(sgu_b[layer].T, gw, axis=1)
        mix_w_s = jnp.einsum("ab,gts->gatbs", eye, tril[:, :ds, :ds]).reshape(
            n_sgu_groups, ts, ts).astype(BF16)
        mix_b_s = jnp.tile(jnp.repeat(sgu_b[layer][:, :ds].T, gw, axis=1), (reps, 1))
        w_gu = jnp.concatenate([w_expert_gate[layer], w_expert_up[layer]], axis=-1).astype(BF16)
        w_dn = w_expert_down[layer].astype(BF16)
        g_fin = row2(final_g) if layer == depth - 1 else None

        k_p, v_p, qb, kb, vb = _qkv_proj(
            xp, weights["g1"], w_qkv, tab_p, rows=proj_rows_p, tiles_per_seq=seq // proj_rows_p,
            n_maps=n_maps, head_dim=head_dim, q_dtype=BF16, name="qkv_prompt")
        o_att_p = _prompt_attention(qb, kb, vb, lam_vecs, g_sub, batch=batch, seq=seq,
                                    n_heads=n_heads, head_dim=head_dim, lam_init=lam_init,
                                    name="attn_prompt")
        x1_p, h2_p, route_p = _mix_merge(
            xp, o_att_p, weights, mix_w_p, mix_b_p, rows=mix_rows_p, chunk=chunk,
            n_sgu_groups=n_sgu_groups, n_groups=n_groups, experts_per_group=experts_per_group,
            emit_vn=False, name="mix_prompt")
        plan = _dispatch_plan(route_p, n_experts, MOE_ROWS)
        y_p = _expert_mlp(h2_p, plan, w_gu, w_dn, rows=MOE_ROWS, name="experts_prompt")
        xp = _combine(x1_p, y_p.reshape(tp + MOE_ROWS, TOP_K * d), route_p, g_fin,
                      rows=_row_tile(tp, OUT_ROWS), name="combine_prompt")

        k_s, v_s, q_s, _, _ = _qkv_proj(
            xs, weights["g1"], w_qkv, tab_s, rows=ts, tiles_per_seq=1, n_maps=n_maps,
            head_dim=head_dim, q_dtype=F32, name="qkv_sample")
        page_ids = (page_table.astype(jnp.int32) + layer * n_phys).reshape(-1)
        o_att_s = _decode_attention(q_s, k_s, v_s, ck, cv, page_ids, lam_vecs, g_sub,
                                    dec_batch=dec_batch, ds=ds, n_pages=n_pages,
                                    lam_init=lam_init, name="attn_sample")
        x1_s, h2_s, route_s, vn_s = _mix_merge(
            xs, o_att_s, weights, mix_w_s, mix_b_s, rows=ts, chunk=ts,
            n_sgu_groups=n_sgu_groups, n_groups=n_groups, experts_per_group=experts_per_group,
            emit_vn=True, name="mix_sample")
        xs = _dense_moe(x1_s, h2_s, route_s, w_gu, w_dn, g_fin, name="moe_sample")

        kp_l.append(k_p.reshape(batch, seq, n_maps, head_dim))
        vp_l.append(v_p.reshape(batch, seq, n_heads, v_dim))
        ks_l.append(k_s.reshape(dec_batch, ds, n_maps, head_dim))
        vs_l.append(v_s.reshape(dec_batch, ds, n_heads, v_dim))
        gs_l.append(vn_s.reshape(dec_batch, ds, width))

    return (xp.reshape(batch, seq, d), xs.reshape(dec_batch, ds, d),
            jnp.stack(kp_l), jnp.stack(vp_l), jnp.stack(ks_l), jnp.stack(vs_l),
            jnp.stack(gs_l))
```

```python
import functools
import math

import jax
import jax.numpy as jnp
from jax import lax
from jax.experimental import pallas as pl
from jax.experimental.pallas import tpu as pltpu

F32 = jnp.float32
BF16 = jnp.bfloat16

EPS = 1e-6
ROPE_THETA = 10000.0
TOP_K = 2
LOG2E = math.log2(math.e)
LANES = 128
SUBLANES = 8
VMEM_LIMIT_BYTES = 56 * 1024 * 1024

PROJ_ROWS = 512
MIX_ROWS = 256
ATT_BLOCK = 1024
ATT_ROW_CHUNK = 256
PAGES_PER_STEP = 8
MOE_ROWS = 256
OUT_ROWS = 512


def _params(n_axes):
    return pltpu.CompilerParams(dimension_semantics=("arbitrary",) * n_axes,
                                vmem_limit_bytes=VMEM_LIMIT_BYTES)


def _rms(x, g):
    return (x * lax.rsqrt(jnp.mean(x * x, axis=-1, keepdims=True) + EPS)) * g


def _log2(n):
    assert n > 0 and n & (n - 1) == 0, n
    return n.bit_length() - 1


def _shr(x, n):
    return lax.shift_right_logical(x, _log2(n))


def _rope_tables(pos, head_dim):
    inv = 1.0 / (ROPE_THETA ** (jnp.arange(0, head_dim, 2, dtype=F32) / head_dim))
    ang = pos[:, None] * inv[None, :]
    cos, sin = jnp.cos(ang), jnp.sin(ang)
    zero = jnp.zeros_like(sin)
    reps = LANES // head_dim
    cos_t = jnp.concatenate([cos, cos] * reps, axis=-1)
    sin_lo = jnp.concatenate([-sin, zero] * reps, axis=-1)
    sin_hi = jnp.concatenate([zero, sin] * reps, axis=-1)
    return cos_t, sin_lo, sin_hi


def _qkv_kernel(x_ref, g_ref, w_ref, cos_ref, slo_ref, shi_ref, kout_ref, vout_ref, q_ref,
                *maybe_bf16_refs, qk_width, head_dim, n_maps, n_heads, q_scale, keys_transposed):
    rows = x_ref.shape[0]
    half = head_dim // 2
    h = _rms(x_ref[...], g_ref[...])
    z = jnp.dot(h.astype(BF16), w_ref[...], preferred_element_type=F32)
    cos, slo, shi = cos_ref[...], slo_ref[...], shi_ref[...]

    def rope(blk):
        return (blk * cos + pltpu.roll(blk, LANES - half, 1) * slo
                + pltpu.roll(blk, half, 1) * shi)

    maps_per_block = LANES // head_dim
    for j in range(qk_width // LANES):
        cols = slice(j * LANES, (j + 1) * LANES)
        q_ref[:, cols] = (rope(z[:, cols]) * q_scale).astype(q_ref.dtype)
        kr = rope(z[:, qk_width + j * LANES: qk_width + (j + 1) * LANES])
        if keys_transposed:
            kt = kr.T
            kout_ref[0, cols, :] = kt
            maybe_bf16_refs[0][0, cols, :] = kt.astype(BF16)
        else:
            for i in range(maps_per_block):
                part = kr if i == 0 else pltpu.roll(kr, LANES - i * head_dim, 1)
                kout_ref[pl.ds(j * maps_per_block + i, rows, stride=n_maps), :] = part[:, :head_dim]
    v = z[:, 2 * qk_width:]
    if keys_transposed:
        maybe_bf16_refs[1][...] = v.astype(BF16)
    for hd in range(n_heads):
        vout_ref[pl.ds(hd, rows, stride=n_heads), :] = v[:, hd * LANES:(hd + 1) * LANES]


def _qkv_proj(x, g, w_qkv, tables, *, rows, tiles_per_seq, n_maps, head_dim, prompt, name):
    t, d = x.shape
    n_heads = n_maps // 2
    qk_width = n_maps * head_dim
    att_width = n_heads * 2 * head_dim
    seq = rows * tiles_per_seq
    row_spec = lambda w: pl.BlockSpec((rows, w), lambda i: (i, 0))
    tab_spec = pl.BlockSpec((rows, LANES), lambda i: (i % tiles_per_seq, 0))
    full = lambda a: pl.BlockSpec(a.shape, lambda i: (0,) * a.ndim)
    v_spec = pl.BlockSpec((rows * n_heads, 2 * head_dim), lambda i: (i, 0))
    v_shape = jax.ShapeDtypeStruct((t * n_heads, 2 * head_dim), F32)
    if prompt:
        kt_spec = pl.BlockSpec((1, qk_width, rows),
                               lambda i: (i // tiles_per_seq, 0, i % tiles_per_seq))
        kt_shape = lambda dt: jax.ShapeDtypeStruct((t // seq, qk_width, seq), dt)
        out_specs = [kt_spec, v_spec, row_spec(qk_width), kt_spec, row_spec(att_width)]
        out_shape = [kt_shape(F32), v_shape, jax.ShapeDtypeStruct((t, qk_width), BF16),
                     kt_shape(BF16), jax.ShapeDtypeStruct((t, att_width), BF16)]
    else:
        out_specs = [pl.BlockSpec((rows * n_maps, head_dim), lambda i: (i, 0)), v_spec,
                     row_spec(qk_width)]
        out_shape = [jax.ShapeDtypeStruct((t * n_maps, head_dim), F32), v_shape,
                     jax.ShapeDtypeStruct((t, qk_width), F32)]
    kern = functools.partial(_qkv_kernel, qk_width=qk_width, head_dim=head_dim, n_maps=n_maps,
                             n_heads=n_heads, q_scale=head_dim ** -0.5 * LOG2E,
                             keys_transposed=prompt)
    return pl.pallas_call(
        kern,
        grid=(t // rows,),
        in_specs=[row_spec(d), full(g), full(w_qkv), tab_spec, tab_spec, tab_spec],
        out_specs=out_specs,
        out_shape=out_shape,
        compiler_params=_params(1),
        name=name,
    )(x, g, w_qkv, *tables)


def _diff_lambda(lq1_ref, lk1_ref, lq2_ref, lk2_ref, lam_init):
    s1 = jnp.sum(lq1_ref[...] * lk1_ref[...], axis=-1, keepdims=True)
    s2 = jnp.sum(lq2_ref[...] * lk2_ref[...], axis=-1, keepdims=True)
    return jnp.exp(s1) - jnp.exp(s2) + lam_init


def _subln(o1, o2, lam, g, lam_init):
    d = o1 - lam * o2
    return _rms(d, g) * (1.0 - lam_init)


def _prompt_attn_kernel(qi_ref, ki_ref, q_ref, k_ref, v_ref, lq1_ref, lk1_ref, lq2_ref,
                        lk2_ref, g_ref, o_ref, q2_ref, m_ref, acc_ref,
                        *, blk, chunk, head_dim, lam_init):
    p = pl.program_id(2)
    qi, ki = qi_ref[p], ki_ref[p]
    v_dim = 2 * head_dim

    @pl.when(ki == 0)
    def _init():
        q = q_ref[...]
        lane = lax.broadcasted_iota(jnp.int32, q.shape, 1)
        zero = jnp.zeros_like(q)
        q2_ref[:blk, :] = jnp.where(lane < head_dim, q, zero)
        q2_ref[blk:, :] = jnp.where(lane >= head_dim, q, zero)
        m_ref[...] = jnp.full(m_ref.shape, -jnp.inf, F32)
        acc_ref[...] = jnp.zeros(acc_ref.shape, F32)

    def step(masked):
        for c in range(2 * blk // chunk):
            rows = slice(c * chunk, (c + 1) * chunk)
            first = (c * chunk) % blk
            n_keys = first + chunk if masked else blk
            v1 = jnp.concatenate([v_ref[:n_keys, :], jnp.ones((n_keys, LANES), BF16)], axis=1)
            s = jnp.dot(q2_ref[rows, :], k_ref[0, :, :n_keys],
                        preferred_element_type=F32)
            if masked:
                r = lax.broadcasted_iota(jnp.int32, s.shape, 0) + first
                col = lax.broadcasted_iota(jnp.int32, s.shape, 1)
                s = jnp.where(col <= r, s, -jnp.inf)
            m_prev = m_ref[rows, :]
            m_new = jnp.maximum(m_prev, jnp.max(s, axis=1, keepdims=True))
            alpha = jnp.exp2(m_prev - m_new)
            pexp = jnp.exp2(s - jnp.concatenate([m_new] * (n_keys // LANES), axis=1))
            acc_ref[rows, :] = (jnp.concatenate([alpha] * (acc_ref.shape[1] // LANES), axis=1)
                                * acc_ref[rows, :]
                                + jnp.dot(pexp.astype(BF16), v1, preferred_element_type=F32))
            m_ref[rows, :] = m_new

    @pl.when(ki < qi)
    def _off_diagonal():
        step(False)

    @pl.when(ki == qi)
    def _diagonal():
        step(True)
        acc = acc_ref[...]
        o = acc[:, :v_dim] / acc[:, v_dim:]
        lam = _diff_lambda(lq1_ref, lk1_ref, lq2_ref, lk2_ref, lam_init)
        o_ref[...] = _subln(o[:blk], o[blk:], lam, g_ref[...], lam_init).astype(o_ref.dtype)


def _prompt_attention(qb, kb, vb, lam_vecs, subln_g, *, batch, seq, n_heads, head_dim,
                      lam_init, name):
    blk = min(ATT_BLOCK, seq)
    chunk = min(ATT_ROW_CHUNK, blk)
    nq = seq // blk
    pairs = [(i, j) for i in range(nq) for j in range(i + 1)]
    qi_map = jnp.asarray([a for a, _ in pairs], jnp.int32)
    ki_map = jnp.asarray([b for _, b in pairs], jnp.int32)
    v_dim = 2 * head_dim
    assert v_dim == LANES
    q_spec = pl.BlockSpec((blk, v_dim), lambda b, h, p, qi, ki: (b * nq + qi[p], h))
    k_spec = pl.BlockSpec((1, v_dim, blk), lambda b, h, p, qi, ki: (b, h, ki[p]))
    v_spec = pl.BlockSpec((blk, v_dim), lambda b, h, p, qi, ki: (b * nq + ki[p], h))
    vec = lambda a: pl.BlockSpec(a.shape, lambda b, h, p, qi, ki: (0, 0))
    kern = functools.partial(_prompt_attn_kernel, blk=blk, chunk=chunk, head_dim=head_dim,
                             lam_init=lam_init)
    return pl.pallas_call(
        kern,
        grid_spec=pltpu.PrefetchScalarGridSpec(
            num_scalar_prefetch=2,
            grid=(batch, n_heads, len(pairs)),
            in_specs=[q_spec, k_spec, v_spec] + [vec(a) for a in lam_vecs] + [vec(subln_g)],
            out_specs=q_spec,
            scratch_shapes=[pltpu.VMEM((2 * blk, v_dim), BF16),
                            pltpu.VMEM((2 * blk, LANES), F32),
                            pltpu.VMEM((2 * blk, v_dim + LANES), F32)]),
        out_shape=jax.ShapeDtypeStruct((batch * seq, n_heads * v_dim), BF16),
        compiler_params=_params(3),
        name=name,
    )(qi_map, ki_map, qb, kb, vb, *lam_vecs, subln_g)


def _decode_attn_kernel(*refs, n_pages_step, page, ds, n_maps, head_dim, lam_init):
    pps = n_pages_step
    q_ref, knew_ref, vnew_ref = refs[1:4]
    k_refs = refs[4:4 + pps]
    v_refs = refs[4 + pps:4 + 2 * pps]
    lq1_ref, lk1_ref, lq2_ref, lk2_ref, g_ref = refs[4 + 2 * pps:9 + 2 * pps]
    o_ref = refs[9 + 2 * pps]
    qs_ref, m_ref, l_ref, acc_ref = refs[10 + 2 * pps:]
    j = pl.program_id(1)
    n_heads = n_maps // 2
    v_dim = 2 * head_dim
    head_rows = 2 * ds

    def head_values(v_ref, h):
        return v_ref[0, pl.ds(h, page, stride=n_heads), :].astype(BF16)

    def update(keys, values, causal):
        s = jnp.dot(qs_ref[...], keys, preferred_element_type=F32)
        if causal:
            qpos = lax.broadcasted_iota(jnp.int32, s.shape, 0) & (ds - 1)
            kpos = lax.broadcasted_iota(jnp.int32, s.shape, 1)
            s = jnp.where(kpos <= qpos, s, -jnp.inf)
        m_prev = m_ref[...]
        m_new = jnp.maximum(m_prev, jnp.max(s, axis=1, keepdims=True))
        alpha = jnp.exp2(m_prev - m_new)
        pexp = jnp.exp2(s - jnp.concatenate([m_new] * (s.shape[1] // LANES), axis=1))
        l_ref[...] = alpha * l_ref[...] + jnp.sum(pexp, axis=1, keepdims=True)
        pb = pexp.astype(BF16)
        for h in range(n_heads):
            r = slice(h * head_rows, (h + 1) * head_rows)
            acc_ref[r, :] = alpha[r] * acc_ref[r, :] + jnp.dot(pb[r], values[h],
                                                               preferred_element_type=F32)
        m_ref[...] = m_new

    @pl.when(j == 0)
    def _init():
        q = q_ref[...]
        qt = jnp.concatenate([q] * n_maps, axis=0)
        r = lax.broadcasted_iota(jnp.int32, qt.shape, 0)
        c = lax.broadcasted_iota(jnp.int32, qt.shape, 1)
        qs_ref[...] = jnp.where(_shr(c, head_dim) == _shr(r, ds), qt,
                                jnp.zeros_like(qt)).astype(BF16)
        m_ref[...] = jnp.full(m_ref.shape, -jnp.inf, F32)
        l_ref[...] = jnp.zeros(l_ref.shape, F32)
        acc_ref[...] = jnp.zeros(acc_ref.shape, F32)
        update(knew_ref[0].astype(BF16), [head_values(vnew_ref, h) for h in range(n_heads)],
               causal=True)

    keys = jnp.concatenate([k_refs[i][0].astype(BF16) for i in range(pps)], axis=1)
    values = [jnp.concatenate([head_values(v_refs[i], h) for i in range(pps)], axis=0)
              for h in range(n_heads)]
    update(keys, values, causal=False)

    @pl.when(j == pl.num_programs(1) - 1)
    def _finish():
        lam = _diff_lambda(lq1_ref, lk1_ref, lq2_ref, lk2_ref, lam_init)
        o = acc_ref[...] / l_ref[...]
        for h in range(n_heads):
            o1 = o[h * head_rows:h * head_rows + ds]
            o2 = o[h * head_rows + ds:(h + 1) * head_rows]
            o_ref[:, h * v_dim:(h + 1) * v_dim] = _subln(o1, o2, lam, g_ref[...],
                                                         lam_init).astype(o_ref.dtype)


def _decode_attention(q, k_new, v_new, cache_kt, cache_v, page_ids, lam_vecs, subln_g, *,
                      dec_batch, ds, n_pages, n_maps, lam_init, name):
    qk_width, page = cache_kt.shape[1:]
    v_dim = cache_v.shape[2]
    n_heads = cache_v.shape[1] // page
    head_dim = qk_width // n_maps
    rows = n_maps * ds
    assert v_dim == LANES and rows % SUBLANES == 0
    pps = math.gcd(PAGES_PER_STEP, n_pages)
    k_new = jnp.pad(k_new.reshape(dec_batch, ds, qk_width),
                    ((0, 0), (0, page - ds), (0, 0))).transpose(0, 2, 1)
    v_new = jnp.pad(v_new.reshape(dec_batch, ds * n_heads, v_dim),
                    ((0, 0), (0, (page - ds) * n_heads), (0, 0)))

    def page_spec(i, shape):
        return pl.BlockSpec((1,) + shape, lambda b, j, pt: (pt[b * n_pages + j * pps + i],)
                            + (0,) * len(shape))

    vec = lambda a: pl.BlockSpec(a.shape, lambda b, j, pt: (0, 0))
    kern = functools.partial(_decode_attn_kernel, n_pages_step=pps, page=page, ds=ds,
                             n_maps=n_maps, head_dim=head_dim, lam_init=lam_init)
    return pl.pallas_call(
        kern,
        grid_spec=pltpu.PrefetchScalarGridSpec(
            num_scalar_prefetch=1,
            grid=(dec_batch, n_pages // pps),
            in_specs=([pl.BlockSpec((ds, qk_width), lambda b, j, pt: (b, 0)),
                       pl.BlockSpec((1, qk_width, page), lambda b, j, pt: (b, 0, 0)),
                       pl.BlockSpec((1, page * n_heads, v_dim), lambda b, j, pt: (b, 0, 0))]
                      + [page_spec(i, (qk_width, page)) for i in range(pps)]
                      + [page_spec(i, (page * n_heads, v_dim)) for i in range(pps)]
                      + [vec(a) for a in lam_vecs] + [vec(subln_g)]),
            out_specs=pl.BlockSpec((ds, n_heads * v_dim), lambda b, j, pt: (b, 0)),
            scratch_shapes=[pltpu.VMEM((rows, qk_width), BF16),
                            pltpu.VMEM((rows, LANES), F32),
                            pltpu.VMEM((rows, LANES), F32),
                            pltpu.VMEM((rows, v_dim), F32)]),
        out_shape=jax.ShapeDtypeStruct((dec_batch * ds, n_heads * v_dim), F32),
        compiler_params=_params(2),
        name=name,
    )(page_ids, q, k_new, v_new, *([cache_kt] * pps), *([cache_v] * pps), *lam_vecs, subln_g)


def _gelu_tanh(x):
    return 0.5 * x * (1.0 + jnp.tanh(math.sqrt(2.0 / math.pi) * (x + 0.044715 * (x * x * x))))


def _mix_kernel(x_ref, oatt_ref, g1_ref, w2_ref, lng_ref, lnb_ref, mixw_ref, mixb_ref,
                wa_ref, ws_ref, wo_ref, g2_ref, wr_ref, br_ref,
                x1_ref, h2_ref, route_ref, *maybe_vn_ref,
                width, n_sgu_groups, chunk, n_groups, experts_per_group):
    x = x_ref[...]
    h = _rms(x, g1_ref[...]).astype(BF16)
    z = jnp.dot(h, w2_ref[...], preferred_element_type=F32)
    u = _gelu_tanh(z[:, :width])
    vg = _gelu_tanh(z[:, width:2 * width])
    mu = jnp.mean(vg, axis=-1, keepdims=True)
    var = jnp.mean(jnp.square(vg - mu), axis=-1, keepdims=True)
    vn = (vg - mu) * lax.rsqrt(var + EPS) * lng_ref[...] + lnb_ref[...]
    if maybe_vn_ref:
        maybe_vn_ref[0][...] = vn
    vnb = vn.astype(BF16)

    rows = x.shape[0]
    gw = width // n_sgu_groups
    mixed_chunks = []
    for c in range(rows // chunk):
        parts = [jnp.dot(mixw_ref[g], vnb[c * chunk:(c + 1) * chunk, g * gw:(g + 1) * gw],
                         preferred_element_type=F32) for g in range(n_sgu_groups)]
        mixed_chunks.append(jnp.concatenate(parts, axis=1) + mixb_ref[...])
    mixed = jnp.concatenate(mixed_chunks, axis=0) if len(mixed_chunks) > 1 else mixed_chunks[0]
    o_sgu = (u * mixed).astype(BF16)

    a_br = jnp.dot(oatt_ref[...].astype(BF16), wa_ref[...], preferred_element_type=F32)
    s_br = jnp.dot(o_sgu, ws_ref[...], preferred_element_type=F32)
    d = x.shape[1]
    m = (jax.nn.sigmoid(z[:, 2 * width:2 * width + d]) * a_br
         + jax.nn.sigmoid(z[:, 2 * width + d:]) * s_br)
    x1 = x + jnp.dot(m.astype(BF16), wo_ref[...], preferred_element_type=F32)
    x1_ref[...] = x1
    h2 = _rms(x1, g2_ref[...])
    h2_ref[...] = h2

    logits = jnp.dot(h2.astype(BF16), wr_ref[...], preferred_element_type=F32) + br_ref[...]
    lane = lax.broadcasted_iota(jnp.int32, logits.shape, 1).astype(F32)
    big = float(LANES)
    is_group = lane < n_groups
    gl = jnp.where(is_group, logits, -jnp.inf)
    gmax = jnp.max(gl, axis=1, keepdims=True)
    g_sel = jnp.min(jnp.where(gl == gmax, lane, big), axis=1, keepdims=True)
    p_g = 1.0 / jnp.sum(jnp.where(is_group, jnp.exp(logits - gmax), 0.0), axis=1, keepdims=True)
    lo = n_groups + g_sel * experts_per_group
    el = jnp.where((lane >= lo) & (lane < lo + experts_per_group), logits, -jnp.inf)
    v1 = jnp.max(el, axis=1, keepdims=True)
    i1 = jnp.min(jnp.where(el == v1, lane, big), axis=1, keepdims=True)
    el2 = jnp.where(lane == i1, -jnp.inf, el)
    v2 = jnp.max(el2, axis=1, keepdims=True)
    i2 = jnp.min(jnp.where(el2 == v2, lane, big), axis=1, keepdims=True)
    t = jnp.exp(v2 - v1)
    w1 = p_g / (1.0 + t)
    w2 = p_g * t / (1.0 + t)
    route_ref[...] = jnp.where(lane == 0, i1 - n_groups,
                     jnp.where(lane == 1, i2 - n_groups,
                     jnp.where(lane == 2, w1,
                     jnp.where(lane == 3, w2, 0.0))))


def _mix_merge(x, o_att, weights, mix_w, mix_b, *, rows, chunk, n_sgu_groups, n_groups,
               experts_per_group, emit_vn, name):
    t, d = x.shape
    width = weights["ln_g"].shape[1]
    row_spec = lambda w: pl.BlockSpec((rows, w), lambda i: (i, 0))
    full = lambda a: pl.BlockSpec(a.shape, lambda i: (0,) * a.ndim)
    names = ["g1", "w2", "ln_g", "ln_b"]
    tail = ["wa", "ws", "wo", "g2", "wr", "br"]
    ins = [x, o_att] + [weights[n] for n in names] + [mix_w, mix_b] + [weights[n] for n in tail]
    in_specs = [row_spec(d), row_spec(o_att.shape[1])] + [full(a) for a in ins[2:]]
    out_specs = [row_spec(d), row_spec(d), row_spec(LANES)]
    out_shape = [jax.ShapeDtypeStruct((t, d), F32), jax.ShapeDtypeStruct((t, d), F32),
                 jax.ShapeDtypeStruct((t, LANES), F32)]
    if emit_vn:
        out_specs.append(row_spec(width))
        out_shape.append(jax.ShapeDtypeStruct((t, width), F32))
    kern = functools.partial(_mix_kernel, width=width, n_sgu_groups=n_sgu_groups, chunk=chunk,
                             n_groups=n_groups, experts_per_group=experts_per_group)
    return pl.pallas_call(
        kern, grid=(t // rows,), in_specs=in_specs, out_specs=out_specs, out_shape=out_shape,
        compiler_params=_params(1), name=name,
    )(*ins)


def _swiglu(xb, wgu, wd, d_expert):
    gu = jnp.dot(xb, wgu, preferred_element_type=F32)
    gate, up = gu[:, :d_expert], gu[:, d_expert:]
    act = (gate * jax.nn.sigmoid(gate)) * up
    return jnp.dot(act.astype(BF16), wd, preferred_element_type=F32)


def _expert_kernel(blk_e_ref, n_active_ref, tok_ref, tok_next_ref, dst_ref, h_hbm, wgu_ref,
                   wd_ref, y_hbm, xbuf, ybuf, gsem, ssem, *, rows, d_expert, n_blocks):
    del blk_e_ref
    j = pl.program_id(0)
    n_active = n_active_ref[0]
    slot = lax.rem(j, 2)

    def start_gather(ids_ref, s):
        for r in range(rows):
            pltpu.make_async_copy(h_hbm.at[pl.ds(ids_ref[0, 0, r], 1)],
                                  xbuf.at[s, pl.ds(r, 1)], gsem.at[s]).start()

    def wait_gather(s):
        pltpu.make_async_copy(h_hbm.at[pl.ds(0, rows)], xbuf.at[s], gsem.at[s]).wait()

    def wait_scatter(s):
        pltpu.make_async_copy(ybuf.at[s], y_hbm.at[pl.ds(0, rows)], ssem.at[s]).wait()

    @pl.when(j == 0)
    def _prime():
        start_gather(tok_ref, 0)
        ybuf[...] = jnp.zeros(ybuf.shape, F32)
        n_real = y_hbm.shape[0] - 2 * rows
        for s in range(2):
            spare = pltpu.make_async_copy(ybuf.at[s], y_hbm.at[pl.ds(n_real + s * rows, rows)],
                                          ssem.at[s])
            spare.start()
            spare.wait()

    @pl.when(j < n_active)
    def _block():
        wait_gather(slot)

        @pl.when(j >= 2)
        def _():
            wait_scatter(slot)

        start_gather(tok_next_ref, 1 - slot)
        ybuf[slot] = _swiglu(xbuf[slot].astype(BF16), wgu_ref[0], wd_ref[0], d_expert)
        for r in range(rows):
            pltpu.make_async_copy(ybuf.at[slot, pl.ds(r, 1)],
                                  y_hbm.at[pl.ds(dst_ref[0, 0, r], 1)], ssem.at[slot]).start()

    @pl.when(j == n_blocks - 1)
    def _drain():
        wait_gather(lax.rem(n_active, 2))

        @pl.when(n_active >= 1)
        def _():
            wait_scatter(lax.rem(n_active - 1, 2))

        @pl.when(n_active >= 2)
        def _():
            wait_scatter(lax.rem(n_active, 2))


def _expert_mlp(h2, plan, w_gu, w_d, *, rows, name):
    blk_e, n_active, tok, dst, n_blocks = plan
    t, d = h2.shape
    d_expert = w_d.shape[1]
    smem = lambda imap: pl.BlockSpec((1, 1, rows), imap, memory_space=pltpu.SMEM)
    kern = functools.partial(_expert_kernel, rows=rows, d_expert=d_expert, n_blocks=n_blocks)
    return pl.pallas_call(
        kern,
        grid_spec=pltpu.PrefetchScalarGridSpec(
            num_scalar_prefetch=2,
            grid=(n_blocks,),
            in_specs=[smem(lambda j, be, na: (j, 0, 0)),
                      smem(lambda j, be, na: (jnp.minimum(j + 1, n_blocks - 1), 0, 0)),
                      smem(lambda j, be, na: (j, 0, 0)),
                      pl.BlockSpec(memory_space=pl.ANY),
                      pl.BlockSpec((1, d, 2 * d_expert), lambda j, be, na: (be[j], 0, 0)),
                      pl.BlockSpec((1, d_expert, d), lambda j, be, na: (be[j], 0, 0))],
            out_specs=pl.BlockSpec(memory_space=pl.ANY),
            scratch_shapes=[pltpu.VMEM((2, rows, d), F32), pltpu.VMEM((2, rows, d), F32),
                            pltpu.SemaphoreType.DMA((2,)), pltpu.SemaphoreType.DMA((2,))]),
        out_shape=jax.ShapeDtypeStruct((TOP_K * t + 2 * rows, d), F32),
        compiler_params=_params(1),
        name=name,
    )(blk_e, n_active, tok, tok, dst, h2, w_gu, w_d)


def _dispatch_plan(route, n_experts, rows):
    t = route.shape[0]
    tk = TOP_K * t
    flat_e = route[:, :TOP_K].astype(jnp.int32).reshape(tk)
    order = jnp.argsort(flat_e, stable=True).astype(jnp.int32)
    experts = jnp.arange(n_experts, dtype=jnp.int32)
    counts = jnp.sum((flat_e[:, None] == experts[None, :]).astype(jnp.int32), axis=0)
    blocks_e = (counts + rows - 1) // rows
    blk_end = jnp.cumsum(blocks_e)
    blk_start = blk_end - blocks_e
    start = jnp.cumsum(counts) - counts
    n_blocks = (tk + n_experts * (rows - 1) + rows - 1) // rows
    j = jnp.arange(n_blocks, dtype=jnp.int32)
    blk_e = jnp.minimum(jnp.sum((j[:, None] >= blk_end[None, :]).astype(jnp.int32), axis=1),
                        n_experts - 1)
    within = j - blk_start[blk_e]
    nvalid = jnp.clip(counts[blk_e] - within * rows, 0, rows)
    r = jnp.arange(rows, dtype=jnp.int32)
    valid = r[None, :] < nvalid[:, None]
    src = jnp.clip(start[blk_e][:, None] + within[:, None] * rows + r[None, :], 0, tk - 1)
    flat = order[src]
    tok = flat // TOP_K
    spare = tk + (j & 1)[:, None] * rows + r[None, :]
    dst = jnp.where(valid, (flat % TOP_K) * t + tok, spare).reshape(n_blocks, 1, rows)
    tok = jnp.where(valid, tok, 0).reshape(n_blocks, 1, rows)
    n_active = blk_end[-1:].astype(jnp.int32)
    return blk_e, n_active, tok, dst, n_blocks


def _combine_kernel(x1_ref, y0_ref, y1_ref, route_ref, *rest):
    o_ref = rest[-1]
    route = route_ref[...]
    out = x1_ref[...] + (y0_ref[...] * route[:, 2:3] + y1_ref[...] * route[:, 3:4])
    if len(rest) == 2:
        out = _rms(out, rest[0][...])
    o_ref[...] = out


def _combine(x1, y, route, final_g, *, rows, name):
    t, d = x1.shape
    row_spec = lambda w: pl.BlockSpec((rows, w), lambda i: (i, 0))
    gains = [] if final_g is None else [final_g]
    return pl.pallas_call(
        _combine_kernel,
        grid=(t // rows,),
        in_specs=[row_spec(d), row_spec(d), pl.BlockSpec((rows, d), lambda i: (i + t // rows, 0)),
                  row_spec(LANES)]
                 + [pl.BlockSpec(g.shape, lambda i: (0, 0)) for g in gains],
        out_specs=row_spec(d),
        out_shape=jax.ShapeDtypeStruct((t, d), F32),
        compiler_params=_params(1),
        name=name,
    )(x1, y, y, route, *gains)


def _dense_moe_kernel(x1_ref, h2_ref, route_ref, wgu_ref, wd_ref, *rest, d_expert):
    o_ref, acc_ref = rest[-2:]
    e = pl.program_id(0)

    @pl.when(e == 0)
    def _():
        acc_ref[...] = jnp.zeros(acc_ref.shape, F32)

    route = route_ref[...]
    ef = e.astype(F32)
    coef = (jnp.where(route[:, 0:1] == ef, route[:, 2:3], 0.0)
            + jnp.where(route[:, 1:2] == ef, route[:, 3:4], 0.0))
    routed = (route[:, 0:1] == ef) | (route[:, 1:2] == ef)
    y = _swiglu(h2_ref[...].astype(BF16), wgu_ref[0], wd_ref[0], d_expert)
    acc_ref[...] += jnp.where(routed, y * coef, 0.0)

    @pl.when(e == pl.num_programs(0) - 1)
    def _():
        out = x1_ref[...] + acc_ref[...]
        if len(rest) == 3:
            out = _rms(out, rest[0][...])
        o_ref[...] = out


def _dense_moe(x1, h2, route, w_gu, w_d, final_g, *, name):
    t, d = x1.shape
    n_experts, d_expert = w_d.shape[:2]
    full = lambda a: pl.BlockSpec(a.shape, lambda e: (0,) * a.ndim)
    gains = [] if final_g is None else [final_g]
    return pl.pallas_call(
        functools.partial(_dense_moe_kernel, d_expert=d_expert),
        grid=(n_experts,),
        in_specs=[full(x1), full(h2), full(route),
                  pl.BlockSpec((1, d, 2 * d_expert), lambda e: (e, 0, 0)),
                  pl.BlockSpec((1, d_expert, d), lambda e: (e, 0, 0))] + [full(g) for g in gains],
        out_specs=full(x1),
        out_shape=jax.ShapeDtypeStruct((t, d), F32),
        scratch_shapes=[pltpu.VMEM((t, d), F32)],
        compiler_params=_params(1),
        name=name,
    )(x1, h2, route, w_gu, w_d, *gains)


def _row_tile(t, want):
    return want if t % want == 0 else t


def kernel(x_prompt, x_sample, cache_k, cache_v, page_table, norm1_g, w_in, lambda_q1, lambda_k1, lambda_q2, lambda_k2, subln_g, sgu_ln_g, sgu_ln_b, sgu_w, sgu_b, w_att_branch, w_sgu_branch, w_out, norm2_g, w_router_group, b_router_group, w_router_expert, b_router_expert, w_expert_gate, w_expert_up, w_expert_down, final_g):
    batch, seq, d = x_prompt.shape
    dec_batch, ds, _ = x_sample.shape
    depth, n_phys, page, n_maps, head_dim = cache_k.shape
    n_heads = n_maps // 2
    v_dim = 2 * head_dim
    qk_width = n_maps * head_dim
    att_width = n_heads * v_dim
    width = sgu_ln_g.shape[-1]
    n_sgu_groups, chunk = sgu_w.shape[1], sgu_w.shape[2]
    n_groups = w_router_group.shape[-1]
    experts_per_group = w_router_expert.shape[-1]
    n_experts = n_groups * experts_per_group
    n_pages = page_table.shape[1]
    past = n_pages * page
    assert 2 * qk_width + att_width + 2 * width + 2 * d == w_in.shape[-1]
    assert v_dim == LANES and seq % chunk == 0 and ds <= chunk and ds <= page
    assert n_groups + n_experts <= LANES

    tp, ts = batch * seq, dec_batch * ds
    xp = x_prompt.reshape(tp, d)
    xs = x_sample.reshape(ts, d)
    tab_p = _rope_tables(jnp.arange(seq, dtype=F32), head_dim)
    tab_s = _rope_tables(jnp.arange(ds, dtype=F32) + float(past), head_dim)
    tab_s = tuple(jnp.tile(a, (dec_batch, 1)) for a in tab_s)
    ckt = cache_k.transpose(0, 1, 3, 4, 2).reshape(depth * n_phys, qk_width, page)
    cv = cache_v.reshape(depth * n_phys, page * n_heads, v_dim)
    row2 = lambda a: a.reshape(1, -1)

    proj_rows_p = _row_tile(seq, PROJ_ROWS)
    mix_rows_p = _row_tile(seq, MIX_ROWS)
    reps = ts // ds
    eye = jnp.eye(reps, dtype=F32)

    kp_l, vp_l, ks_l, vs_l, gs_l = [], [], [], [], []
    for layer in range(depth):
        lam_init = 0.8 - 0.6 * math.exp(-0.3 * layer)
        w_l = w_in[layer].astype(BF16)
        w_qkv = w_l[:, :2 * qk_width + att_width]
        lam_vecs = [row2(a[layer]) for a in (lambda_q1, lambda_k1, lambda_q2, lambda_k2)]
        g_sub = row2(subln_g[layer])
        tril = jnp.tril(sgu_w[layer])
        w_route = jnp.concatenate(
            [w_router_group[layer],
             jnp.transpose(w_router_expert[layer], (1, 0, 2)).reshape(d, n_experts)], axis=1)
        b_route = jnp.concatenate([b_router_group[layer], b_router_expert[layer].reshape(-1)])
        lane_pad = LANES - (n_groups + n_experts)
        weights = dict(
            g1=row2(norm1_g[layer]), w2=w_l[:, 2 * qk_width + att_width:],
            ln_g=row2(sgu_ln_g[layer]), ln_b=row2(sgu_ln_b[layer]),
            wa=w_att_branch[layer].astype(BF16), ws=w_sgu_branch[layer].astype(BF16),
            wo=w_out[layer].astype(BF16), g2=row2(norm2_g[layer]),
            wr=jnp.pad(w_route, ((0, 0), (0, lane_pad))).astype(BF16),
            br=row2(jnp.pad(b_route, (0, lane_pad))))
        gw = width // n_sgu_groups
        mix_w_p = tril.astype(BF16)
        mix_b_p = jnp.repeat(sgu_b[layer].T, gw, axis=1)
        mix_w_s = jnp.einsum("ab,gts->gatbs", eye, tril[:, :ds, :ds]).reshape(
            n_sgu_groups, ts, ts).astype(BF16)
        mix_b_s = jnp.tile(jnp.repeat(sgu_b[layer][:, :ds].T, gw, axis=1), (reps, 1))
        w_gu = jnp.concatenate([w_expert_gate[layer], w_expert_up[layer]], axis=-1).astype(BF16)
        w_dn = w_expert_down[layer].astype(BF16)
        g_fin = row2(final_g) if layer == depth - 1 else None

        kt_p, v_p, qb, kbt, vb = _qkv_proj(
            xp, weights["g1"], w_qkv, tab_p, rows=proj_rows_p, tiles_per_seq=seq // proj_rows_p,
            n_maps=n_maps, head_dim=head_dim, prompt=True, name="qkv_prompt")
        o_att_p = _prompt_attention(qb, kbt, vb, lam_vecs, g_sub, batch=batch, seq=seq,
                                    n_heads=n_heads, head_dim=head_dim, lam_init=lam_init,
                                    name="attn_prompt")
        x1_p, h2_p, route_p = _mix_merge(
            xp, o_att_p, weights, mix_w_p, mix_b_p, rows=mix_rows_p, chunk=chunk,
            n_sgu_groups=n_sgu_groups, n_groups=n_groups, experts_per_group=experts_per_group,
            emit_vn=False, name="mix_prompt")
        plan = _dispatch_plan(route_p, n_experts, MOE_ROWS)
        y_p = _expert_mlp(h2_p, plan, w_gu, w_dn, rows=MOE_ROWS, name="experts_prompt")
        xp = _combine(x1_p, y_p, route_p, g_fin, rows=_row_tile(tp, OUT_ROWS),
                      name="combine_prompt")

        k_s, v_s, q_s = _qkv_proj(
            xs, weights["g1"], w_qkv, tab_s, rows=ts, tiles_per_seq=1, n_maps=n_maps,
            head_dim=head_dim, prompt=False, name="qkv_sample")
        page_ids = (page_table.astype(jnp.int32) + layer * n_phys).reshape(-1)
        o_att_s = _decode_attention(q_s, k_s, v_s, ckt, cv, page_ids, lam_vecs, g_sub,
                                    dec_batch=dec_batch, ds=ds, n_pages=n_pages, n_maps=n_maps,
                                    lam_init=lam_init, name="attn_sample")
        x1_s, h2_s, route_s, vn_s = _mix_merge(
            xs, o_att_s, weights, mix_w_s, mix_b_s, rows=ts, chunk=ts,
            n_sgu_groups=n_sgu_groups, n_groups=n_groups, experts_per_group=experts_per_group,
            emit_vn=True, name="mix_sample")
        xs = _dense_moe(x1_s, h2_s, route_s, w_gu, w_dn, g_fin, name="moe_sample")

        kp_l.append(kt_p.reshape(batch, n_maps, head_dim, seq).transpose(0, 3, 1, 2))
        vp_l.append(v_p.reshape(batch, seq, n_heads, v_dim))
        ks_l.append(k_s.reshape(dec_batch, ds, n_maps, head_dim))
        vs_l.append(v_s.reshape(dec_batch, ds, n_heads, v_dim))
        gs_l.append(vn_s.reshape(dec_batch, ds, width))

    return (xp.reshape(batch, seq, d), xs.reshape(dec_batch, ds, d),
            jnp.stack(kp_l), jnp.stack(vp_l), jnp.stack(ks_l), jnp.stack(vs_l),
            jnp.stack(gs_l))
```

```python
import functools
import math

import jax
import jax.numpy as jnp
from jax import lax
from jax.experimental import pallas as pl
from jax.experimental.pallas import tpu as pltpu

F32 = jnp.float32
BF16 = jnp.bfloat16

EPS = 1e-6
ROPE_THETA = 10000.0
TOP_K = 2
LOG2E = math.log2(math.e)
LANES = 128
SUBLANES = 8
VMEM_LIMIT_BYTES = 56 * 1024 * 1024

PROJ_ROWS = 512
MIX_ROWS = 256
ATT_BLOCK = 2048
ATT_ROW_CHUNK = 512
PAGES_PER_STEP = 8
MOE_ROWS = 256
OUT_ROWS = 512


def _params(n_axes):
    return pltpu.CompilerParams(dimension_semantics=("arbitrary",) * n_axes,
                                vmem_limit_bytes=VMEM_LIMIT_BYTES)


def _rms(x, g):
    return (x * lax.rsqrt(jnp.mean(x * x, axis=-1, keepdims=True) + EPS)) * g


def _log2(n):
    assert n > 0 and n & (n - 1) == 0, n
    return n.bit_length() - 1


def _shr(x, n):
    return lax.shift_right_logical(x, _log2(n))


def _rope_tables(pos, head_dim):
    inv = 1.0 / (ROPE_THETA ** (jnp.arange(0, head_dim, 2, dtype=F32) / head_dim))
    ang = pos[:, None] * inv[None, :]
    cos, sin = jnp.cos(ang), jnp.sin(ang)
    zero = jnp.zeros_like(sin)
    reps = LANES // head_dim
    cos_t = jnp.concatenate([cos, cos] * reps, axis=-1)
    sin_lo = jnp.concatenate([-sin, zero] * reps, axis=-1)
    sin_hi = jnp.concatenate([zero, sin] * reps, axis=-1)
    return cos_t, sin_lo, sin_hi


def _qkv_kernel(x_ref, g_ref, w_ref, cos_ref, slo_ref, shi_ref, kout_ref, vout_ref, q_ref,
                *maybe_bf16_refs, qk_width, head_dim, n_maps, n_heads, q_scale, keys_transposed):
    rows = x_ref.shape[0]
    half = head_dim // 2
    h = _rms(x_ref[...], g_ref[...])
    z = jnp.dot(h.astype(BF16), w_ref[...], preferred_element_type=F32)
    cos, slo, shi = cos_ref[...], slo_ref[...], shi_ref[...]

    def rope(blk):
        return (blk * cos + pltpu.roll(blk, LANES - half, 1) * slo
                + pltpu.roll(blk, half, 1) * shi)

    maps_per_block = LANES // head_dim
    for j in range(qk_width // LANES):
        cols = slice(j * LANES, (j + 1) * LANES)
        q_ref[:, cols] = (rope(z[:, cols]) * q_scale).astype(q_ref.dtype)
        kr = rope(z[:, qk_width + j * LANES: qk_width + (j + 1) * LANES])
        if keys_transposed:
            kout_ref[0, cols, :] = kr.T
            maybe_bf16_refs[0][:, cols] = kr.astype(BF16)
        else:
            for i in range(maps_per_block):
                part = kr if i == 0 else pltpu.roll(kr, LANES - i * head_dim, 1)
                kout_ref[pl.ds(j * maps_per_block + i, rows, stride=n_maps), :] = part[:, :head_dim]
    v = z[:, 2 * qk_width:]
    if keys_transposed:
        maybe_bf16_refs[1][...] = v.astype(BF16)
    for hd in range(n_heads):
        vout_ref[pl.ds(hd, rows, stride=n_heads), :] = v[:, hd * LANES:(hd + 1) * LANES]


def _qkv_proj(x, g, w_qkv, tables, *, rows, tiles_per_seq, n_maps, head_dim, prompt, name):
    t, d = x.shape
    n_heads = n_maps // 2
    qk_width = n_maps * head_dim
    att_width = n_heads * 2 * head_dim
    seq = rows * tiles_per_seq
    row_spec = lambda w: pl.BlockSpec((rows, w), lambda i: (i, 0))
    tab_spec = pl.BlockSpec((rows, LANES), lambda i: (i % tiles_per_seq, 0))
    full = lambda a: pl.BlockSpec(a.shape, lambda i: (0,) * a.ndim)
    v_spec = pl.BlockSpec((rows * n_heads, 2 * head_dim), lambda i: (i, 0))
    v_shape = jax.ShapeDtypeStruct((t * n_heads, 2 * head_dim), F32)
    if prompt:
        kt_spec = pl.BlockSpec((1, qk_width, rows),
                               lambda i: (i // tiles_per_seq, 0, i % tiles_per_seq))
        out_specs = [kt_spec, v_spec, row_spec(qk_width), row_spec(qk_width), row_spec(att_width)]
        out_shape = [jax.ShapeDtypeStruct((t // seq, qk_width, seq), F32), v_shape,
                     jax.ShapeDtypeStruct((t, qk_width), BF16),
                     jax.ShapeDtypeStruct((t, qk_width), BF16),
                     jax.ShapeDtypeStruct((t, att_width), BF16)]
    else:
        out_specs = [pl.BlockSpec((rows * n_maps, head_dim), lambda i: (i, 0)), v_spec,
                     row_spec(qk_width)]
        out_shape = [jax.ShapeDtypeStruct((t * n_maps, head_dim), F32), v_shape,
                     jax.ShapeDtypeStruct((t, qk_width), F32)]
    kern = functools.partial(_qkv_kernel, qk_width=qk_width, head_dim=head_dim, n_maps=n_maps,
                             n_heads=n_heads, q_scale=head_dim ** -0.5 * (LOG2E if prompt else 1.0),
                             keys_transposed=prompt)
    return pl.pallas_call(
        kern,
        grid=(t // rows,),
        in_specs=[row_spec(d), full(g), full(w_qkv), tab_spec, tab_spec, tab_spec],
        out_specs=out_specs,
        out_shape=out_shape,
        compiler_params=_params(1),
        name=name,
    )(x, g, w_qkv, *tables)


def _diff_lambda(lq1_ref, lk1_ref, lq2_ref, lk2_ref, lam_init):
    s1 = jnp.sum(lq1_ref[...] * lk1_ref[...], axis=-1, keepdims=True)
    s2 = jnp.sum(lq2_ref[...] * lk2_ref[...], axis=-1, keepdims=True)
    return jnp.exp(s1) - jnp.exp(s2) + lam_init


def _subln(o1, o2, lam, g, lam_init):
    d = o1 - lam * o2
    return _rms(d, g) * (1.0 - lam_init)


def _prompt_attn_kernel(qi_ref, ki_ref, q_ref, k_ref, v_ref, lq1_ref, lk1_ref, lq2_ref,
                        lk2_ref, g_ref, o_ref, q2_ref, m_ref, acc_ref,
                        *, blk, chunk, head_dim, lam_init):
    p = pl.program_id(2)
    qi, ki = qi_ref[p], ki_ref[p]
    v_dim = 2 * head_dim

    @pl.when(ki == 0)
    def _init():
        q = q_ref[...]
        lane = lax.broadcasted_iota(jnp.int32, q.shape, 1)
        zero = jnp.zeros_like(q)
        q2_ref[:blk, :] = jnp.where(lane < head_dim, q, zero)
        q2_ref[blk:, :] = jnp.where(lane >= head_dim, q, zero)
        m_ref[...] = jnp.full(m_ref.shape, -jnp.inf, F32)
        acc_ref[...] = jnp.zeros(acc_ref.shape, F32)

    def step(masked):
        for c in range(2 * blk // chunk):
            rows = slice(c * chunk, (c + 1) * chunk)
            first = (c * chunk) % blk
            n_keys = first + chunk if masked else blk
            v1 = jnp.concatenate([v_ref[:n_keys, :], jnp.ones((n_keys, LANES), BF16)], axis=1)
            s = lax.dot_general(q2_ref[rows, :], k_ref[:n_keys, :], (((1,), (1,)), ((), ())),
                                preferred_element_type=F32)
            if masked:
                r = lax.broadcasted_iota(jnp.int32, s.shape, 0) + first
                col = lax.broadcasted_iota(jnp.int32, s.shape, 1)
                s = jnp.where(col <= r, s, -jnp.inf)
            m_prev = m_ref[rows, :]
            m_new = jnp.maximum(m_prev, jnp.max(s, axis=1, keepdims=True))
            alpha = jnp.exp2(m_prev - m_new)
            pexp = jnp.exp2(s - jnp.concatenate([m_new] * (n_keys // LANES), axis=1))
            acc_ref[rows, :] = (jnp.concatenate([alpha] * (acc_ref.shape[1] // LANES), axis=1)
                                * acc_ref[rows, :]
                                + jnp.dot(pexp.astype(BF16), v1, preferred_element_type=F32))
            m_ref[rows, :] = m_new

    @pl.when(ki < qi)
    def _off_diagonal():
        step(False)

    @pl.when(ki == qi)
    def _diagonal():
        step(True)
        acc = acc_ref[...]
        o = acc[:, :v_dim] / acc[:, v_dim:]
        lam = _diff_lambda(lq1_ref, lk1_ref, lq2_ref, lk2_ref, lam_init)
        o_ref[...] = _subln(o[:blk], o[blk:], lam, g_ref[...], lam_init).astype(o_ref.dtype)


def _prompt_attention(qb, kb, vb, lam_vecs, subln_g, *, batch, seq, n_heads, head_dim,
                      lam_init, name):
    blk = min(ATT_BLOCK, seq)
    chunk = min(ATT_ROW_CHUNK, blk)
    nq = seq // blk
    pairs = [(i, j) for i in range(nq) for j in range(i + 1)]
    qi_map = jnp.asarray([a for a, _ in pairs], jnp.int32)
    ki_map = jnp.asarray([b for _, b in pairs], jnp.int32)
    v_dim = 2 * head_dim
    assert v_dim == LANES
    q_spec = pl.BlockSpec((blk, v_dim), lambda b, h, p, qi, ki: (b * nq + qi[p], h))
    kv_spec = pl.BlockSpec((blk, v_dim), lambda b, h, p, qi, ki: (b * nq + ki[p], h))
    vec = lambda a: pl.BlockSpec(a.shape, lambda b, h, p, qi, ki: (0, 0))
    kern = functools.partial(_prompt_attn_kernel, blk=blk, chunk=chunk, head_dim=head_dim,
                             lam_init=lam_init)
    return pl.pallas_call(
        kern,
        grid_spec=pltpu.PrefetchScalarGridSpec(
            num_scalar_prefetch=2,
            grid=(batch, n_heads, len(pairs)),
            in_specs=[q_spec, kv_spec, kv_spec] + [vec(a) for a in lam_vecs] + [vec(subln_g)],
            out_specs=q_spec,
            scratch_shapes=[pltpu.VMEM((2 * blk, v_dim), BF16),
                            pltpu.VMEM((2 * blk, LANES), F32),
                            pltpu.VMEM((2 * blk, v_dim + LANES), F32)]),
        out_shape=jax.ShapeDtypeStruct((batch * seq, n_heads * v_dim), BF16),
        compiler_params=_params(3),
        name=name,
    )(qi_map, ki_map, qb, kb, vb, *lam_vecs, subln_g)


def _decode_attn_kernel(*refs, n_pages_step, page, ds, n_maps, head_dim, lam_init):
    pps = n_pages_step
    q_ref, knew_ref, vnew_ref = refs[1:4]
    k_refs = refs[4:4 + pps]
    v_refs = refs[4 + pps:4 + 2 * pps]
    lq1_ref, lk1_ref, lq2_ref, lk2_ref, g_ref = refs[4 + 2 * pps:9 + 2 * pps]
    o_ref = refs[9 + 2 * pps]
    qs_ref, m_ref, l_ref, acc_ref = refs[10 + 2 * pps:]
    j = pl.program_id(1)
    n_heads = n_maps // 2
    v_dim = 2 * head_dim
    head_rows = 2 * ds

    def head_values(v_ref, h):
        return v_ref[0, pl.ds(h, page, stride=n_heads), :].astype(BF16)

    def update(keys, values, causal):
        s = jnp.dot(qs_ref[...], keys, preferred_element_type=F32)
        if causal:
            qpos = lax.broadcasted_iota(jnp.int32, s.shape, 0) & (ds - 1)
            kpos = lax.broadcasted_iota(jnp.int32, s.shape, 1)
            s = jnp.where(kpos <= qpos, s, -jnp.inf)
        m_prev = m_ref[...]
        m_new = jnp.maximum(m_prev, jnp.max(s, axis=1, keepdims=True))
        alpha = jnp.exp(m_prev - m_new)
        pexp = jnp.exp(s - jnp.concatenate([m_new] * (s.shape[1] // LANES), axis=1))
        l_ref[...] = alpha * l_ref[...] + jnp.sum(pexp, axis=1, keepdims=True)
        p_hi = pexp.astype(BF16)
        p_lo = (pexp - p_hi.astype(F32)).astype(BF16)
        for h in range(n_heads):
            r = slice(h * head_rows, (h + 1) * head_rows)
            pv = (jnp.dot(p_hi[r], values[h], preferred_element_type=F32)
                  + jnp.dot(p_lo[r], values[h], preferred_element_type=F32))
            acc_ref[r, :] = alpha[r] * acc_ref[r, :] + pv
        m_ref[...] = m_new

    @pl.when(j == 0)
    def _init():
        q = q_ref[...]
        qt = jnp.concatenate([q] * n_maps, axis=0)
        r = lax.broadcasted_iota(jnp.int32, qt.shape, 0)
        c = lax.broadcasted_iota(jnp.int32, qt.shape, 1)
        qs_ref[...] = jnp.where(_shr(c, head_dim) == _shr(r, ds), qt,
                                jnp.zeros_like(qt)).astype(BF16)
        m_ref[...] = jnp.full(m_ref.shape, -jnp.inf, F32)
        l_ref[...] = jnp.zeros(l_ref.shape, F32)
        acc_ref[...] = jnp.zeros(acc_ref.shape, F32)
        update(knew_ref[0].astype(BF16), [head_values(vnew_ref, h) for h in range(n_heads)],
               causal=True)

    keys = jnp.concatenate([k_refs[i][0].astype(BF16) for i in range(pps)], axis=1)
    values = [jnp.concatenate([head_values(v_refs[i], h) for i in range(pps)], axis=0)
              for h in range(n_heads)]
    update(keys, values, causal=False)

    @pl.when(j == pl.num_programs(1) - 1)
    def _finish():
        lam = _diff_lambda(lq1_ref, lk1_ref, lq2_ref, lk2_ref, lam_init)
        o = acc_ref[...] / l_ref[...]
        for h in range(n_heads):
            o1 = o[h * head_rows:h * head_rows + ds]
            o2 = o[h * head_rows + ds:(h + 1) * head_rows]
            o_ref[:, h * v_dim:(h + 1) * v_dim] = _subln(o1, o2, lam, g_ref[...],
                                                         lam_init).astype(o_ref.dtype)


def _decode_attention(q, k_new, v_new, cache_kt, cache_v, page_ids, lam_vecs, subln_g, *,
                      dec_batch, ds, n_pages, n_maps, lam_init, name):
    qk_width, page = cache_kt.shape[1:]
    v_dim = cache_v.shape[2]
    n_heads = cache_v.shape[1] // page
    head_dim = qk_width // n_maps
    rows = n_maps * ds
    assert v_dim == LANES and rows % SUBLANES == 0
    pps = math.gcd(PAGES_PER_STEP, n_pages)
    k_new = jnp.pad(k_new.reshape(dec_batch, ds, qk_width),
                    ((0, 0), (0, page - ds), (0, 0))).transpose(0, 2, 1)
    v_new = jnp.pad(v_new.reshape(dec_batch, ds * n_heads, v_dim),
                    ((0, 0), (0, (page - ds) * n_heads), (0, 0)))

    def page_spec(i, shape):
        return pl.BlockSpec((1,) + shape, lambda b, j, pt: (pt[b * n_pages + j * pps + i],)
                            + (0,) * len(shape))

    vec = lambda a: pl.BlockSpec(a.shape, lambda b, j, pt: (0, 0))
    kern = functools.partial(_decode_attn_kernel, n_pages_step=pps, page=page, ds=ds,
                             n_maps=n_maps, head_dim=head_dim, lam_init=lam_init)
    return pl.pallas_call(
        kern,
        grid_spec=pltpu.PrefetchScalarGridSpec(
            num_scalar_prefetch=1,
            grid=(dec_batch, n_pages // pps),
            in_specs=([pl.BlockSpec((ds, qk_width), lambda b, j, pt: (b, 0)),
                       pl.BlockSpec((1, qk_width, page), lambda b, j, pt: (b, 0, 0)),
                       pl.BlockSpec((1, page * n_heads, v_dim), lambda b, j, pt: (b, 0, 0))]
                      + [page_spec(i, (qk_width, page)) for i in range(pps)]
                      + [page_spec(i, (page * n_heads, v_dim)) for i in range(pps)]
                      + [vec(a) for a in lam_vecs] + [vec(subln_g)]),
            out_specs=pl.BlockSpec((ds, n_heads * v_dim), lambda b, j, pt: (b, 0)),
            scratch_shapes=[pltpu.VMEM((rows, qk_width), BF16),
                            pltpu.VMEM((rows, LANES), F32),
                            pltpu.VMEM((rows, LANES), F32),
                            pltpu.VMEM((rows, v_dim), F32)]),
        out_shape=jax.ShapeDtypeStruct((dec_batch * ds, n_heads * v_dim), F32),
        compiler_params=_params(2),
        name=name,
    )(page_ids, q, k_new, v_new, *([cache_kt] * pps), *([cache_v] * pps), *lam_vecs, subln_g)


def _store_row_tiles(ref, x):
    rows, k = x.shape[0], x.shape[1] // LANES
    for c in range(k):
        ref[pl.ds(c, rows, stride=k), :] = x[:, c * LANES:(c + 1) * LANES]


def _load_row_tiles(ref, rows):
    k = ref.shape[0] // rows
    return jnp.concatenate([ref[pl.ds(c, rows, stride=k), :] for c in range(k)], axis=1)


def _gelu_tanh(x):
    return 0.5 * x * (1.0 + jnp.tanh(math.sqrt(2.0 / math.pi) * (x + 0.044715 * (x * x * x))))


def _mix_kernel(x_ref, oatt_ref, g1_ref, w2_ref, lng_ref, lnb_ref, mixw_ref, mixb_ref,
                wa_ref, ws_ref, wo_ref, g2_ref, wr_ref, br_ref,
                x1_ref, h2_ref, route_ref, *maybe_vn_ref,
                width, n_sgu_groups, chunk, n_groups, experts_per_group):
    x = x_ref[...]
    h = _rms(x, g1_ref[...]).astype(BF16)
    z = jnp.dot(h, w2_ref[...], preferred_element_type=F32)
    u = _gelu_tanh(z[:, :width])
    vg = _gelu_tanh(z[:, width:2 * width])
    mu = jnp.mean(vg, axis=-1, keepdims=True)
    var = jnp.mean(jnp.square(vg - mu), axis=-1, keepdims=True)
    vn = (vg - mu) * lax.rsqrt(var + EPS) * lng_ref[...] + lnb_ref[...]
    if maybe_vn_ref:
        maybe_vn_ref[0][...] = vn
    vnb = vn.astype(BF16)

    rows = x.shape[0]
    gw = width // n_sgu_groups
    mixed_chunks = []
    for c in range(rows // chunk):
        parts = [jnp.dot(mixw_ref[g], vnb[c * chunk:(c + 1) * chunk, g * gw:(g + 1) * gw],
                         preferred_element_type=F32) for g in range(n_sgu_groups)]
        mixed_chunks.append(jnp.concatenate(parts, axis=1) + mixb_ref[...])
    mixed = jnp.concatenate(mixed_chunks, axis=0) if len(mixed_chunks) > 1 else mixed_chunks[0]
    o_sgu = (u * mixed).astype(BF16)

    a_br = jnp.dot(oatt_ref[...].astype(BF16), wa_ref[...], preferred_element_type=F32)
    s_br = jnp.dot(o_sgu, ws_ref[...], preferred_element_type=F32)
    d = x.shape[1]
    m = (jax.nn.sigmoid(z[:, 2 * width:2 * width + d]) * a_br
         + jax.nn.sigmoid(z[:, 2 * width + d:]) * s_br)
    x1 = x + jnp.dot(m.astype(BF16), wo_ref[...], preferred_element_type=F32)
    x1_ref[...] = x1
    h2 = _rms(x1, g2_ref[...])
    if h2_ref.shape == x1.shape:
        h2_ref[...] = h2
    else:
        _store_row_tiles(h2_ref, h2)

    logits = jnp.dot(h2.astype(BF16), wr_ref[...], preferred_element_type=F32) + br_ref[...]
    lane = lax.broadcasted_iota(jnp.int32, logits.shape, 1).astype(F32)
    big = float(LANES)
    is_group = lane < n_groups
    gl = jnp.where(is_group, logits, -jnp.inf)
    gmax = jnp.max(gl, axis=1, keepdims=True)
    g_sel = jnp.min(jnp.where(gl == gmax, lane, big), axis=1, keepdims=True)
    p_g = 1.0 / jnp.sum(jnp.where(is_group, jnp.exp(logits - gmax), 0.0), axis=1, keepdims=True)
    lo = n_groups + g_sel * experts_per_group
    el = jnp.where((lane >= lo) & (lane < lo + experts_per_group), logits, -jnp.inf)
    v1 = jnp.max(el, axis=1, keepdims=True)
    i1 = jnp.min(jnp.where(el == v1, lane, big), axis=1, keepdims=True)
    el2 = jnp.where(lane == i1, -jnp.inf, el)
    v2 = jnp.max(el2, axis=1, keepdims=True)
    i2 = jnp.min(jnp.where(el2 == v2, lane, big), axis=1, keepdims=True)
    t = jnp.exp(v2 - v1)
    w1 = p_g / (1.0 + t)
    w2 = p_g * t / (1.0 + t)
    route_ref[...] = jnp.where(lane == 0, i1 - n_groups,
                     jnp.where(lane == 1, i2 - n_groups,
                     jnp.where(lane == 2, w1,
                     jnp.where(lane == 3, w2, 0.0))))


def _mix_merge(x, o_att, weights, mix_w, mix_b, *, rows, chunk, n_sgu_groups, n_groups,
               experts_per_group, prompt, name):
    t, d = x.shape
    width = weights["ln_g"].shape[1]
    row_spec = lambda w: pl.BlockSpec((rows, w), lambda i: (i, 0))
    full = lambda a: pl.BlockSpec(a.shape, lambda i: (0,) * a.ndim)
    names = ["g1", "w2", "ln_g", "ln_b"]
    tail = ["wa", "ws", "wo", "g2", "wr", "br"]
    ins = [x, o_att] + [weights[n] for n in names] + [mix_w, mix_b] + [weights[n] for n in tail]
    in_specs = [row_spec(d), row_spec(o_att.shape[1])] + [full(a) for a in ins[2:]]
    k = d // LANES
    h2_spec = pl.BlockSpec((rows * k, LANES), lambda i: (i, 0)) if prompt else row_spec(d)
    h2_shape = (t * k, LANES) if prompt else (t, d)
    out_specs = [row_spec(d), h2_spec, row_spec(LANES)]
    out_shape = [jax.ShapeDtypeStruct((t, d), F32), jax.ShapeDtypeStruct(h2_shape, F32),
                 jax.ShapeDtypeStruct((t, LANES), F32)]
    if not prompt:
        out_specs.append(row_spec(width))
        out_shape.append(jax.ShapeDtypeStruct((t, width), F32))
    kern = functools.partial(_mix_kernel, width=width, n_sgu_groups=n_sgu_groups, chunk=chunk,
                             n_groups=n_groups, experts_per_group=experts_per_group)
    return pl.pallas_call(
        kern, grid=(t // rows,), in_specs=in_specs, out_specs=out_specs, out_shape=out_shape,
        compiler_params=_params(1), name=name,
    )(*ins)


def _swiglu(xb, wgu, wd, d_expert):
    gu = jnp.dot(xb, wgu, preferred_element_type=F32)
    gate, up = gu[:, :d_expert], gu[:, d_expert:]
    act = (gate * jax.nn.sigmoid(gate)) * up
    return jnp.dot(act.astype(BF16), wd, preferred_element_type=F32)


def _expert_kernel(blk_e_ref, n_active_ref, tok_ref, tok_next_ref, dst_ref, h_hbm, wgu_ref,
                   wd_ref, y_hbm, xbuf, ybuf, gsem, ssem, *, rows, d_expert, n_blocks):
    del blk_e_ref
    j = pl.program_id(0)
    n_active = n_active_ref[0]
    slot = lax.rem(j, 2)
    k = xbuf.shape[1] // rows

    def start_gather(ids_ref, s):
        for r in range(rows):
            src = pl.multiple_of(ids_ref[0, 0, r], k)
            pltpu.make_async_copy(h_hbm.at[pl.ds(src, k)], xbuf.at[s, pl.ds(r * k, k)],
                                  gsem.at[s]).start()

    def wait_gather(s):
        pltpu.make_async_copy(h_hbm.at[pl.ds(0, rows * k)], xbuf.at[s], gsem.at[s]).wait()

    def wait_scatter(s):
        pltpu.make_async_copy(ybuf.at[s], y_hbm.at[pl.ds(0, rows * k)], ssem.at[s]).wait()

    @pl.when(j == 0)
    def _prime():
        start_gather(tok_ref, 0)
        ybuf[...] = jnp.zeros(ybuf.shape, F32)
        n_real = y_hbm.shape[0] - 2 * rows * k
        for s in range(2):
            spare = pltpu.make_async_copy(
                ybuf.at[s], y_hbm.at[pl.ds(n_real + s * rows * k, rows * k)], ssem.at[s])
            spare.start()
            spare.wait()

    @pl.when(j < n_active)
    def _block():
        wait_gather(slot)

        @pl.when(j >= 2)
        def _():
            wait_scatter(slot)

        start_gather(tok_next_ref, 1 - slot)
        x = _load_row_tiles(xbuf.at[slot], rows).astype(BF16)
        _store_row_tiles(ybuf.at[slot], _swiglu(x, wgu_ref[0], wd_ref[0], d_expert))
        for r in range(rows):
            dst = pl.multiple_of(dst_ref[0, 0, r], k)
            pltpu.make_async_copy(ybuf.at[slot, pl.ds(r * k, k)], y_hbm.at[pl.ds(dst, k)],
                                  ssem.at[slot]).start()

    @pl.when(j == n_blocks - 1)
    def _drain():
        wait_gather(lax.rem(n_active, 2))

        @pl.when(n_active >= 1)
        def _():
            wait_scatter(lax.rem(n_active - 1, 2))

        @pl.when(n_active >= 2)
        def _():
            wait_scatter(lax.rem(n_active, 2))


def _expert_mlp(h2, plan, w_gu, w_d, *, rows, name):
    blk_e, n_active, tok, dst, n_blocks = plan
    d_expert, d = w_d.shape[1:]
    k = d // LANES
    t = h2.shape[0] // k
    smem = lambda imap: pl.BlockSpec((1, 1, rows), imap, memory_space=pltpu.SMEM)
    kern = functools.partial(_expert_kernel, rows=rows, d_expert=d_expert, n_blocks=n_blocks)
    return pl.pallas_call(
        kern,
        grid_spec=pltpu.PrefetchScalarGridSpec(
            num_scalar_prefetch=2,
            grid=(n_blocks,),
            in_specs=[smem(lambda j, be, na: (j, 0, 0)),
                      smem(lambda j, be, na: (jnp.minimum(j + 1, n_blocks - 1), 0, 0)),
                      smem(lambda j, be, na: (j, 0, 0)),
                      pl.BlockSpec(memory_space=pl.ANY),
                      pl.BlockSpec((1, d, 2 * d_expert), lambda j, be, na: (be[j], 0, 0)),
                      pl.BlockSpec((1, d_expert, d), lambda j, be, na: (be[j], 0, 0))],
            out_specs=pl.BlockSpec(memory_space=pl.ANY),
            scratch_shapes=[pltpu.VMEM((2, rows * k, LANES), F32),
                            pltpu.VMEM((2, rows * k, LANES), F32),
                            pltpu.SemaphoreType.DMA((2,)), pltpu.SemaphoreType.DMA((2,))]),
        out_shape=jax.ShapeDtypeStruct(((TOP_K * t + 2 * rows) * k, LANES), F32),
        compiler_params=_params(1),
        name=name,
    )(blk_e, n_active, tok, tok, dst, h2, w_gu, w_d)


def _dispatch_plan(route, n_experts, rows, k):
    t = route.shape[0]
    tk = TOP_K * t
    flat_e = route[:, :TOP_K].astype(jnp.int32).reshape(tk)
    order = jnp.argsort(flat_e, stable=True).astype(jnp.int32)
    experts = jnp.arange(n_experts, dtype=jnp.int32)
    counts = jnp.sum((flat_e[:, None] == experts[None, :]).astype(jnp.int32), axis=0)
    blocks_e = (counts + rows - 1) // rows
    blk_end = jnp.cumsum(blocks_e)
    blk_start = blk_end - blocks_e
    start = jnp.cumsum(counts) - counts
    n_blocks = (tk + n_experts * (rows - 1) + rows - 1) // rows
    j = jnp.arange(n_blocks, dtype=jnp.int32)
    blk_e = jnp.minimum(jnp.sum((j[:, None] >= blk_end[None, :]).astype(jnp.int32), axis=1),
                        n_experts - 1)
    within = j - blk_start[blk_e]
    nvalid = jnp.clip(counts[blk_e] - within * rows, 0, rows)
    r = jnp.arange(rows, dtype=jnp.int32)
    valid = r[None, :] < nvalid[:, None]
    src = jnp.clip(start[blk_e][:, None] + within[:, None] * rows + r[None, :], 0, tk - 1)
    flat = order[src]
    tok = flat // TOP_K
    spare = tk + (j & 1)[:, None] * rows + r[None, :]
    dst = (jnp.where(valid, (flat % TOP_K) * t + tok, spare) * k).reshape(n_blocks, 1, rows)
    tok = (jnp.where(valid, tok, 0) * k).reshape(n_blocks, 1, rows)
    n_active = blk_end[-1:].astype(jnp.int32)
    return blk_e, n_active, tok, dst, n_blocks


def _combine_kernel(x1_ref, y0_ref, y1_ref, route_ref, *rest):
    o_ref = rest[-1]
    route = route_ref[...]
    rows = route.shape[0]
    out = x1_ref[...] + (_load_row_tiles(y0_ref, rows) * route[:, 2:3]
                         + _load_row_tiles(y1_ref, rows) * route[:, 3:4])
    if len(rest) == 2:
        out = _rms(out, rest[0][...])
    o_ref[...] = out


def _combine(x1, y, route, final_g, *, rows, name):
    t, d = x1.shape
    k = d // LANES
    row_spec = lambda w: pl.BlockSpec((rows, w), lambda i: (i, 0))
    gains = [] if final_g is None else [final_g]
    return pl.pallas_call(
        _combine_kernel,
        grid=(t // rows,),
        in_specs=[row_spec(d), pl.BlockSpec((rows * k, LANES), lambda i: (i, 0)),
                  pl.BlockSpec((rows * k, LANES), lambda i: (i + t // rows, 0)),
                  row_spec(LANES)]
                 + [pl.BlockSpec(g.shape, lambda i: (0, 0)) for g in gains],
        out_specs=row_spec(d),
        out_shape=jax.ShapeDtypeStruct((t, d), F32),
        compiler_params=_params(1),
        name=name,
    )(x1, y, y, route, *gains)


def _dense_moe_kernel(x1_ref, h2_ref, route_ref, wgu_ref, wd_ref, *rest, d_expert):
    o_ref, acc_ref = rest[-2:]
    e = pl.program_id(0)

    @pl.when(e == 0)
    def _():
        acc_ref[...] = jnp.zeros(acc_ref.shape, F32)

    route = route_ref[...]
    ef = e.astype(F32)
    coef = (jnp.where(route[:, 0:1] == ef, route[:, 2:3], 0.0)
            + jnp.where(route[:, 1:2] == ef, route[:, 3:4], 0.0))
    routed = (route[:, 0:1] == ef) | (route[:, 1:2] == ef)
    y = _swiglu(h2_ref[...].astype(BF16), wgu_ref[0], wd_ref[0], d_expert)
    acc_ref[...] += jnp.where(routed, y * coef, 0.0)

    @pl.when(e == pl.num_programs(0) - 1)
    def _():
        out = x1_ref[...] + acc_ref[...]
        if len(rest) == 3:
            out = _rms(out, rest[0][...])
        o_ref[...] = out


def _dense_moe(x1, h2, route, w_gu, w_d, final_g, *, name):
    t, d = x1.shape
    n_experts, d_expert = w_d.shape[:2]
    full = lambda a: pl.BlockSpec(a.shape, lambda e: (0,) * a.ndim)
    gains = [] if final_g is None else [final_g]
    return pl.pallas_call(
        functools.partial(_dense_moe_kernel, d_expert=d_expert),
        grid=(n_experts,),
        in_specs=[full(x1), full(h2), full(route),
                  pl.BlockSpec((1, d, 2 * d_expert), lambda e: (e, 0, 0)),
                  pl.BlockSpec((1, d_expert, d), lambda e: (e, 0, 0))] + [full(g) for g in gains],
        out_specs=full(x1),
        out_shape=jax.ShapeDtypeStruct((t, d), F32),
        scratch_shapes=[pltpu.VMEM((t, d), F32)],
        compiler_params=_params(1),
        name=name,
    )(x1, h2, route, w_gu, w_d, *gains)


def _row_tile(t, want):
    return want if t % want == 0 else t


def kernel(x_prompt, x_sample, cache_k, cache_v, page_table, norm1_g, w_in, lambda_q1, lambda_k1, lambda_q2, lambda_k2, subln_g, sgu_ln_g, sgu_ln_b, sgu_w, sgu_b, w_att_branch, w_sgu_branch, w_out, norm2_g, w_router_group, b_router_group, w_router_expert, b_router_expert, w_expert_gate, w_expert_up, w_expert_down, final_g):
    batch, seq, d = x_prompt.shape
    dec_batch, ds, _ = x_sample.shape
    depth, n_phys, page, n_maps, head_dim = cache_k.shape
    n_heads = n_maps // 2
    v_dim = 2 * head_dim
    qk_width = n_maps * head_dim
    att_width = n_heads * v_dim
    width = sgu_ln_g.shape[-1]
    n_sgu_groups, chunk = sgu_w.shape[1], sgu_w.shape[2]
    n_groups = w_router_group.shape[-1]
    experts_per_group = w_router_expert.shape[-1]
    n_experts = n_groups * experts_per_group
    n_pages = page_table.shape[1]
    past = n_pages * page
    assert 2 * qk_width + att_width + 2 * width + 2 * d == w_in.shape[-1]
    assert v_dim == LANES and seq % chunk == 0 and ds <= chunk and ds <= page
    assert n_groups + n_experts <= LANES

    tp, ts = batch * seq, dec_batch * ds
    xp = x_prompt.reshape(tp, d)
    xs = x_sample.reshape(ts, d)
    tab_p = _rope_tables(jnp.arange(seq, dtype=F32), head_dim)
    tab_s = _rope_tables(jnp.arange(ds, dtype=F32) + float(past), head_dim)
    tab_s = tuple(jnp.tile(a, (dec_batch, 1)) for a in tab_s)
    ckt = cache_k.transpose(0, 1, 3, 4, 2).reshape(depth * n_phys, qk_width, page)
    cv = cache_v.reshape(depth * n_phys, page * n_heads, v_dim)
    row2 = lambda a: a.reshape(1, -1)

    proj_rows_p = _row_tile(seq, PROJ_ROWS)
    mix_rows_p = _row_tile(seq, MIX_ROWS)
    reps = ts // ds
    eye = jnp.eye(reps, dtype=F32)

    kp_l, vp_l, ks_l, vs_l, gs_l = [], [], [], [], []
    for layer in range(depth):
        lam_init = 0.8 - 0.6 * math.exp(-0.3 * layer)
        w_l = w_in[layer].astype(BF16)
        w_qkv = w_l[:, :2 * qk_width + att_width]
        lam_vecs = [row2(a[layer]) for a in (lambda_q1, lambda_k1, lambda_q2, lambda_k2)]
        g_sub = row2(subln_g[layer])
        tril = jnp.tril(sgu_w[layer])
        w_route = jnp.concatenate(
            [w_router_group[layer],
             jnp.transpose(w_router_expert[layer], (1, 0, 2)).reshape(d, n_experts)], axis=1)
        b_route = jnp.concatenate([b_router_group[layer], b_router_expert[layer].reshape(-1)])
        lane_pad = LANES - (n_groups + n_experts)
        weights = dict(
            g1=row2(norm1_g[layer]), w2=w_l[:, 2 * qk_width + att_width:],
            ln_g=row2(sgu_ln_g[layer]), ln_b=row2(sgu_ln_b[layer]),
            wa=w_att_branch[layer].astype(BF16), ws=w_sgu_branch[layer].astype(BF16),
            wo=w_out[layer].astype(BF16), g2=row2(norm2_g[layer]),
            wr=jnp.pad(w_route, ((0, 0), (0, lane_pad))).astype(BF16),
            br=row2(jnp.pad(b_route, (0, lane_pad))))
        gw = width // n_sgu_groups
        mix_w_p = tril.astype(BF16)
        mix_b_p = jnp.repeat(sgu_b[layer].T, gw, axis=1)
        mix_w_s = jnp.einsum("ab,gts->gatbs", eye, tril[:, :ds, :ds]).reshape(
            n_sgu_groups, ts, ts).astype(BF16)
        mix_b_s = jnp.tile(jnp.repeat(sgu_b[layer][:, :ds].T, gw, axis=1), (reps, 1))
        w_gu = jnp.concatenate([w_expert_gate[layer], w_expert_up[layer]], axis=-1).astype(BF16)
        w_dn = w_expert_down[layer].astype(BF16)
        g_fin = row2(final_g) if layer == depth - 1 else None

        kt_p, v_p, qb, kb, vb = _qkv_proj(
            xp, weights["g1"], w_qkv, tab_p, rows=proj_rows_p, tiles_per_seq=seq // proj_rows_p,
            n_maps=n_maps, head_dim=head_dim, prompt=True, name="qkv_prompt")
        o_att_p = _prompt_attention(qb, kb, vb, lam_vecs, g_sub, batch=batch, seq=seq,
                                    n_heads=n_heads, head_dim=head_dim, lam_init=lam_init,
                                    name="attn_prompt")
        x1_p, h2_p, route_p = _mix_merge(
            xp, o_att_p, weights, mix_w_p, mix_b_p, rows=mix_rows_p, chunk=chunk,
            n_sgu_groups=n_sgu_groups, n_groups=n_groups, experts_per_group=experts_per_group,
            prompt=True, name="mix_prompt")
        plan = _dispatch_plan(route_p, n_experts, MOE_ROWS, d // LANES)
        y_p = _expert_mlp(h2_p, plan, w_gu, w_dn, rows=MOE_ROWS, name="experts_prompt")
        xp = _combine(x1_p, y_p, route_p, g_fin, rows=_row_tile(tp, OUT_ROWS),
                      name="combine_prompt")

        k_s, v_s, q_s = _qkv_proj(
            xs, weights["g1"], w_qkv, tab_s, rows=ts, tiles_per_seq=1, n_maps=n_maps,
            head_dim=head_dim, prompt=False, name="qkv_sample")
        page_ids = (page_table.astype(jnp.int32) + layer * n_phys).reshape(-1)
        o_att_s = _decode_attention(q_s, k_s, v_s, ckt, cv, page_ids, lam_vecs, g_sub,
                                    dec_batch=dec_batch, ds=ds, n_pages=n_pages, n_maps=n_maps,
                                    lam_init=lam_init, name="attn_sample")
        x1_s, h2_s, route_s, vn_s = _mix_merge(
            xs, o_att_s, weights, mix_w_s, mix_b_s, rows=ts, chunk=ts,
            n_sgu_groups=n_sgu_groups, n_groups=n_groups, experts_per_group=experts_per_group,
            prompt=False, name="mix_sample")
        xs = _dense_moe(x1_s, h2_s, route_s, w_gu, w_dn, g_fin, name="moe_sample")

        kp_l.append(kt_p.reshape(batch, n_maps, head_dim, seq).transpose(0, 3, 1, 2))
        vp_l.append(v_p.reshape(batch, seq, n_heads, v_dim))
        ks_l.append(k_s.reshape(dec_batch, ds, n_maps, head_dim))
        vs_l.append(v_s.reshape(dec_batch, ds, n_heads, v_dim))
        gs_l.append(vn_s.reshape(dec_batch, ds, width))

    return (xp.reshape(batch, seq, d), xs.reshape(dec_batch, ds, d),
            jnp.stack(kp_l), jnp.stack(vp_l), jnp.stack(ks_l), jnp.stack(vs_l),
            jnp.stack(gs_l))
```

```python
import functools
import math

import jax
import jax.numpy as jnp
from jax import lax
from jax.experimental import pallas as pl
from jax.experimental.pallas import tpu as pltpu

F32 = jnp.float32
BF16 = jnp.bfloat16

EPS = 1e-6
ROPE_THETA = 10000.0
TOP_K = 2
LOG2E = math.log2(math.e)
LANES = 128
SUBLANES = 8
VMEM_LIMIT_BYTES = 56 * 1024 * 1024

PROJ_ROWS = 512
MIX_ROWS = 256
ATT_BLOCK = 2048
ATT_ROW_CHUNK = 512
PAGES_PER_STEP = 8
MOE_ROWS = 256
OUT_ROWS = 512


def _params(n_axes):
    return pltpu.CompilerParams(dimension_semantics=("arbitrary",) * n_axes,
                                vmem_limit_bytes=VMEM_LIMIT_BYTES)


def _rms(x, g):
    return (x * lax.rsqrt(jnp.mean(x * x, axis=-1, keepdims=True) + EPS)) * g


def _log2(n):
    assert n > 0 and n & (n - 1) == 0, n
    return n.bit_length() - 1


def _shr(x, n):
    return lax.shift_right_logical(x, _log2(n))


def _rope_tables(pos, head_dim):
    inv = 1.0 / (ROPE_THETA ** (jnp.arange(0, head_dim, 2, dtype=F32) / head_dim))
    ang = pos[:, None] * inv[None, :]
    cos, sin = jnp.cos(ang), jnp.sin(ang)
    zero = jnp.zeros_like(sin)
    reps = LANES // head_dim
    cos_t = jnp.concatenate([cos, cos] * reps, axis=-1)
    sin_lo = jnp.concatenate([-sin, zero] * reps, axis=-1)
    sin_hi = jnp.concatenate([zero, sin] * reps, axis=-1)
    return cos_t, sin_lo, sin_hi


def _qkv_kernel(x_ref, g_ref, w_ref, cos_ref, slo_ref, shi_ref, kout_ref, vout_ref, q_ref,
                *maybe_bf16_refs, qk_width, head_dim, n_maps, n_heads, q_scale, keys_transposed):
    rows = x_ref.shape[0]
    half = head_dim // 2
    h = _rms(x_ref[...], g_ref[...])
    z = jnp.dot(h.astype(BF16), w_ref[...], preferred_element_type=F32)
    cos, slo, shi = cos_ref[...], slo_ref[...], shi_ref[...]

    def rope(blk):
        return (blk * cos + pltpu.roll(blk, LANES - half, 1) * slo
                + pltpu.roll(blk, half, 1) * shi)

    maps_per_block = LANES // head_dim
    for j in range(qk_width // LANES):
        cols = slice(j * LANES, (j + 1) * LANES)
        q_ref[:, cols] = (rope(z[:, cols]) * q_scale).astype(q_ref.dtype)
        kr = rope(z[:, qk_width + j * LANES: qk_width + (j + 1) * LANES])
        if keys_transposed:
            kout_ref[0, cols, :] = kr.T
            maybe_bf16_refs[0][:, cols] = kr.astype(BF16)
        else:
            for i in range(maps_per_block):
                part = kr if i == 0 else pltpu.roll(kr, LANES - i * head_dim, 1)
                kout_ref[pl.ds(j * maps_per_block + i, rows, stride=n_maps), :] = part[:, :head_dim]
    v = z[:, 2 * qk_width:]
    if keys_transposed:
        maybe_bf16_refs[1][...] = v.astype(BF16)
    for hd in range(n_heads):
        vout_ref[pl.ds(hd, rows, stride=n_heads), :] = v[:, hd * LANES:(hd + 1) * LANES]


def _qkv_proj(x, g, w_qkv, tables, *, rows, tiles_per_seq, n_maps, head_dim, prompt, name):
    t, d = x.shape
    n_heads = n_maps // 2
    qk_width = n_maps * head_dim
    att_width = n_heads * 2 * head_dim
    seq = rows * tiles_per_seq
    row_spec = lambda w: pl.BlockSpec((rows, w), lambda i: (i, 0))
    tab_spec = pl.BlockSpec((rows, LANES), lambda i: (i % tiles_per_seq, 0))
    full = lambda a: pl.BlockSpec(a.shape, lambda i: (0,) * a.ndim)
    v_spec = pl.BlockSpec((rows * n_heads, 2 * head_dim), lambda i: (i, 0))
    v_shape = jax.ShapeDtypeStruct((t * n_heads, 2 * head_dim), F32)
    if prompt:
        kt_spec = pl.BlockSpec((1, qk_width, rows),
                               lambda i: (i // tiles_per_seq, 0, i % tiles_per_seq))
        out_specs = [kt_spec, v_spec, row_spec(qk_width), row_spec(qk_width), row_spec(att_width)]
        out_shape = [jax.ShapeDtypeStruct((t // seq, qk_width, seq), F32), v_shape,
                     jax.ShapeDtypeStruct((t, qk_width), BF16),
                     jax.ShapeDtypeStruct((t, qk_width), BF16),
                     jax.ShapeDtypeStruct((t, att_width), BF16)]
    else:
        out_specs = [pl.BlockSpec((rows * n_maps, head_dim), lambda i: (i, 0)), v_spec,
                     row_spec(qk_width)]
        out_shape = [jax.ShapeDtypeStruct((t * n_maps, head_dim), F32), v_shape,
                     jax.ShapeDtypeStruct((t, qk_width), F32)]
    kern = functools.partial(_qkv_kernel, qk_width=qk_width, head_dim=head_dim, n_maps=n_maps,
                             n_heads=n_heads, q_scale=head_dim ** -0.5 * (LOG2E if prompt else 1.0),
                             keys_transposed=prompt)
    return pl.pallas_call(
        kern,
        grid=(t // rows,),
        in_specs=[row_spec(d), full(g), full(w_qkv), tab_spec, tab_spec, tab_spec],
        out_specs=out_specs,
        out_shape=out_shape,
        compiler_params=_params(1),
        name=name,
    )(x, g, w_qkv, *tables)


def _diff_lambda(lq1_ref, lk1_ref, lq2_ref, lk2_ref, lam_init):
    s1 = jnp.sum(lq1_ref[...] * lk1_ref[...], axis=-1, keepdims=True)
    s2 = jnp.sum(lq2_ref[...] * lk2_ref[...], axis=-1, keepdims=True)
    return jnp.exp(s1) - jnp.exp(s2) + lam_init


def _subln(o1, o2, lam, g, lam_init):
    d = o1 - lam * o2
    return _rms(d, g) * (1.0 - lam_init)


def _prompt_attn_kernel(qi_ref, ki_ref, q_ref, k_ref, v_ref, lq1_ref, lk1_ref, lq2_ref,
                        lk2_ref, g_ref, o_ref, q2_ref, m_ref, acc_ref,
                        *, blk, chunk, head_dim, lam_init):
    p = pl.program_id(2)
    qi, ki = qi_ref[p], ki_ref[p]
    v_dim = 2 * head_dim

    @pl.when(ki == 0)
    def _init():
        q = q_ref[...]
        lane = lax.broadcasted_iota(jnp.int32, q.shape, 1)
        zero = jnp.zeros_like(q)
        q2_ref[:blk, :] = jnp.where(lane < head_dim, q, zero)
        q2_ref[blk:, :] = jnp.where(lane >= head_dim, q, zero)
        m_ref[...] = jnp.full(m_ref.shape, -jnp.inf, F32)
        acc_ref[...] = jnp.zeros(acc_ref.shape, F32)

    def step(masked):
        for c in range(2 * blk // chunk):
            rows = slice(c * chunk, (c + 1) * chunk)
            first = (c * chunk) % blk
            n_keys = first + chunk if masked else blk
            v1 = jnp.concatenate([v_ref[:n_keys, :], jnp.ones((n_keys, LANES), BF16)], axis=1)
            s = lax.dot_general(q2_ref[rows, :], k_ref[:n_keys, :], (((1,), (1,)), ((), ())),
                                preferred_element_type=F32)
            if masked:
                r = lax.broadcasted_iota(jnp.int32, s.shape, 0) + first
                col = lax.broadcasted_iota(jnp.int32, s.shape, 1)
                s = jnp.where(col <= r, s, -jnp.inf)
            m_prev = m_ref[rows, :]
            m_new = jnp.maximum(m_prev, jnp.max(s, axis=1, keepdims=True))
            alpha = jnp.exp2(m_prev - m_new)
            pexp = jnp.exp2(s - jnp.concatenate([m_new] * (n_keys // LANES), axis=1))
            acc_ref[rows, :] = (jnp.concatenate([alpha] * (acc_ref.shape[1] // LANES), axis=1)
                                * acc_ref[rows, :]
                                + jnp.dot(pexp.astype(BF16), v1, preferred_element_type=F32))
            m_ref[rows, :] = m_new

    @pl.when(ki < qi)
    def _off_diagonal():
        step(False)

    @pl.when(ki == qi)
    def _diagonal():
        step(True)
        acc = acc_ref[...]
        o = acc[:, :v_dim] / acc[:, v_dim:]
        lam = _diff_lambda(lq1_ref, lk1_ref, lq2_ref, lk2_ref, lam_init)
        o_ref[...] = _subln(o[:blk], o[blk:], lam, g_ref[...], lam_init).astype(o_ref.dtype)


def _prompt_attention(qb, kb, vb, lam_vecs, subln_g, *, batch, seq, n_heads, head_dim,
                      lam_init, name):
    blk = min(ATT_BLOCK, seq)
    chunk = min(ATT_ROW_CHUNK, blk)
    nq = seq // blk
    pairs = [(i, j) for i in range(nq) for j in range(i + 1)]
    qi_map = jnp.asarray([a for a, _ in pairs], jnp.int32)
    ki_map = jnp.asarray([b for _, b in pairs], jnp.int32)
    v_dim = 2 * head_dim
    assert v_dim == LANES
    q_spec = pl.BlockSpec((blk, v_dim), lambda b, h, p, qi, ki: (b * nq + qi[p], h))
    kv_spec = pl.BlockSpec((blk, v_dim), lambda b, h, p, qi, ki: (b * nq + ki[p], h))
    vec = lambda a: pl.BlockSpec(a.shape, lambda b, h, p, qi, ki: (0, 0))
    kern = functools.partial(_prompt_attn_kernel, blk=blk, chunk=chunk, head_dim=head_dim,
                             lam_init=lam_init)
    return pl.pallas_call(
        kern,
        grid_spec=pltpu.PrefetchScalarGridSpec(
            num_scalar_prefetch=2,
            grid=(batch, n_heads, len(pairs)),
            in_specs=[q_spec, kv_spec, kv_spec] + [vec(a) for a in lam_vecs] + [vec(subln_g)],
            out_specs=q_spec,
            scratch_shapes=[pltpu.VMEM((2 * blk, v_dim), BF16),
                            pltpu.VMEM((2 * blk, LANES), F32),
                            pltpu.VMEM((2 * blk, v_dim + LANES), F32)]),
        out_shape=jax.ShapeDtypeStruct((batch * seq, n_heads * v_dim), BF16),
        compiler_params=_params(3),
        name=name,
    )(qi_map, ki_map, qb, kb, vb, *lam_vecs, subln_g)


def _decode_attn_kernel(*refs, n_pages_step, page, ds, n_maps, head_dim, lam_init):
    pps = n_pages_step
    q_ref, knew_ref, vnew_ref = refs[1:4]
    k_refs = refs[4:4 + pps]
    v_refs = refs[4 + pps:4 + 2 * pps]
    lq1_ref, lk1_ref, lq2_ref, lk2_ref, g_ref = refs[4 + 2 * pps:9 + 2 * pps]
    o_ref = refs[9 + 2 * pps]
    qs_ref, m_ref, l_ref, acc_ref = refs[10 + 2 * pps:]
    j = pl.program_id(1)
    n_heads = n_maps // 2
    v_dim = 2 * head_dim
    head_rows = 2 * ds

    def head_values(v_ref, h):
        return v_ref[0, pl.ds(h, page, stride=n_heads), :].astype(BF16)

    def update(keys, values, causal):
        s = jnp.dot(qs_ref[...], keys, preferred_element_type=F32)
        if causal:
            qpos = lax.broadcasted_iota(jnp.int32, s.shape, 0) & (ds - 1)
            kpos = lax.broadcasted_iota(jnp.int32, s.shape, 1)
            s = jnp.where(kpos <= qpos, s, -jnp.inf)
        m_prev = m_ref[...]
        m_new = jnp.maximum(m_prev, jnp.max(s, axis=1, keepdims=True))
        alpha = jnp.exp(m_prev - m_new)
        pexp = jnp.exp(s - jnp.concatenate([m_new] * (s.shape[1] // LANES), axis=1))
        l_ref[...] = alpha * l_ref[...] + jnp.sum(pexp, axis=1, keepdims=True)
        p_hi = pexp.astype(BF16)
        p_lo = (pexp - p_hi.astype(F32)).astype(BF16)
        for h in range(n_heads):
            r = slice(h * head_rows, (h + 1) * head_rows)
            pv = jnp.dot(jnp.concatenate([p_hi[r], p_lo[r]], axis=0), values[h],
                         preferred_element_type=F32)
            acc_ref[r, :] = alpha[r] * acc_ref[r, :] + (pv[:head_rows] + pv[head_rows:])
        m_ref[...] = m_new

    @pl.when(j == 0)
    def _init():
        q = q_ref[...]
        qt = jnp.concatenate([q] * n_maps, axis=0)
        r = lax.broadcasted_iota(jnp.int32, qt.shape, 0)
        c = lax.broadcasted_iota(jnp.int32, qt.shape, 1)
        qs_ref[...] = jnp.where(_shr(c, head_dim) == _shr(r, ds), qt,
                                jnp.zeros_like(qt)).astype(BF16)
        m_ref[...] = jnp.full(m_ref.shape, -jnp.inf, F32)
        l_ref[...] = jnp.zeros(l_ref.shape, F32)
        acc_ref[...] = jnp.zeros(acc_ref.shape, F32)
        update(knew_ref[0].astype(BF16), [head_values(vnew_ref, h) for h in range(n_heads)],
               causal=True)

    keys = jnp.concatenate([k_refs[i][0].astype(BF16) for i in range(pps)], axis=1)
    values = [jnp.concatenate([head_values(v_refs[i], h) for i in range(pps)], axis=0)
              for h in range(n_heads)]
    update(keys, values, causal=False)

    @pl.when(j == pl.num_programs(1) - 1)
    def _finish():
        lam = _diff_lambda(lq1_ref, lk1_ref, lq2_ref, lk2_ref, lam_init)
        o = acc_ref[...] / l_ref[...]
        for h in range(n_heads):
            o1 = o[h * head_rows:h * head_rows + ds]
            o2 = o[h * head_rows + ds:(h + 1) * head_rows]
            o_ref[:, h * v_dim:(h + 1) * v_dim] = _subln(o1, o2, lam, g_ref[...],
                                                         lam_init).astype(o_ref.dtype)


def _decode_attention(q, k_new, v_new, cache_kt, cache_v, page_ids, lam_vecs, subln_g, *,
                      dec_batch, ds, n_pages, n_maps, lam_init, name):
    qk_width, page = cache_kt.shape[1:]
    v_dim = cache_v.shape[2]
    n_heads = cache_v.shape[1] // page
    head_dim = qk_width // n_maps
    rows = n_maps * ds
    assert v_dim == LANES and rows % SUBLANES == 0
    pps = math.gcd(PAGES_PER_STEP, n_pages)
    k_new = jnp.pad(k_new.reshape(dec_batch, ds, qk_width),
                    ((0, 0), (0, page - ds), (0, 0))).transpose(0, 2, 1)
    v_new = jnp.pad(v_new.reshape(dec_batch, ds * n_heads, v_dim),
                    ((0, 0), (0, (page - ds) * n_heads), (0, 0)))

    def page_spec(i, shape):
        return pl.BlockSpec((1,) + shape, lambda b, j, pt: (pt[b * n_pages + j * pps + i],)
                            + (0,) * len(shape))

    vec = lambda a: pl.BlockSpec(a.shape, lambda b, j, pt: (0, 0))
    kern = functools.partial(_decode_attn_kernel, n_pages_step=pps, page=page, ds=ds,
                             n_maps=n_maps, head_dim=head_dim, lam_init=lam_init)
    return pl.pallas_call(
        kern,
        grid_spec=pltpu.PrefetchScalarGridSpec(
            num_scalar_prefetch=1,
            grid=(dec_batch, n_pages // pps),
            in_specs=([pl.BlockSpec((ds, qk_width), lambda b, j, pt: (b, 0)),
                       pl.BlockSpec((1, qk_width, page), lambda b, j, pt: (b, 0, 0)),
                       pl.BlockSpec((1, page * n_heads, v_dim), lambda b, j, pt: (b, 0, 0))]
                      + [page_spec(i, (qk_width, page)) for i in range(pps)]
                      + [page_spec(i, (page * n_heads, v_dim)) for i in range(pps)]
                      + [vec(a) for a in lam_vecs] + [vec(subln_g)]),
            out_specs=pl.BlockSpec((ds, n_heads * v_dim), lambda b, j, pt: (b, 0)),
            scratch_shapes=[pltpu.VMEM((rows, qk_width), BF16),
                            pltpu.VMEM((rows, LANES), F32),
                            pltpu.VMEM((rows, LANES), F32),
                            pltpu.VMEM((rows, v_dim), F32)]),
        out_shape=jax.ShapeDtypeStruct((dec_batch * ds, n_heads * v_dim), F32),
        compiler_params=_params(2),
        name=name,
    )(page_ids, q, k_new, v_new, *([cache_kt] * pps), *([cache_v] * pps), *lam_vecs, subln_g)


def _store_row_tiles(ref, x):
    rows, k = x.shape[0], x.shape[1] // LANES
    for c in range(k):
        ref[pl.ds(c, rows, stride=k), :] = x[:, c * LANES:(c + 1) * LANES]


def _load_row_tiles(ref, rows):
    k = ref.shape[0] // rows
    return jnp.concatenate([ref[pl.ds(c, rows, stride=k), :] for c in range(k)], axis=1)


def _gelu_tanh(x):
    return 0.5 * x * (1.0 + jnp.tanh(math.sqrt(2.0 / math.pi) * (x + 0.044715 * (x * x * x))))


def _mix_kernel(x_ref, oatt_ref, g1_ref, w2_ref, lng_ref, lnb_ref, mixw_ref, mixb_ref,
                wa_ref, ws_ref, wo_ref, g2_ref, wr_ref, br_ref,
                x1_ref, h2_ref, route_ref, *maybe_vn_ref,
                width, n_sgu_groups, chunk, n_groups, experts_per_group):
    x = x_ref[...]
    h = _rms(x, g1_ref[...]).astype(BF16)
    z = jnp.dot(h, w2_ref[...], preferred_element_type=F32)
    u = _gelu_tanh(z[:, :width])
    vg = _gelu_tanh(z[:, width:2 * width])
    mu = jnp.mean(vg, axis=-1, keepdims=True)
    var = jnp.mean(jnp.square(vg - mu), axis=-1, keepdims=True)
    vn = (vg - mu) * lax.rsqrt(var + EPS) * lng_ref[...] + lnb_ref[...]
    if maybe_vn_ref:
        maybe_vn_ref[0][...] = vn
    vnb = vn.astype(BF16)

    rows = x.shape[0]
    gw = width // n_sgu_groups
    mixed_chunks = []
    for c in range(rows // chunk):
        parts = [jnp.dot(mixw_ref[g], vnb[c * chunk:(c + 1) * chunk, g * gw:(g + 1) * gw],
                         preferred_element_type=F32) for g in range(n_sgu_groups)]
        mixed_chunks.append(jnp.concatenate(parts, axis=1) + mixb_ref[...])
    mixed = jnp.concatenate(mixed_chunks, axis=0) if len(mixed_chunks) > 1 else mixed_chunks[0]
    o_sgu = (u * mixed).astype(BF16)

    a_br = jnp.dot(oatt_ref[...].astype(BF16), wa_ref[...], preferred_element_type=F32)
    s_br = jnp.dot(o_sgu, ws_ref[...], preferred_element_type=F32)
    d = x.shape[1]
    m = (jax.nn.sigmoid(z[:, 2 * width:2 * width + d]) * a_br
         + jax.nn.sigmoid(z[:, 2 * width + d:]) * s_br)
    x1 = x + jnp.dot(m.astype(BF16), wo_ref[...], preferred_element_type=F32)
    x1_ref[...] = x1
    h2 = _rms(x1, g2_ref[...])
    if h2_ref.shape == x1.shape:
        h2_ref[...] = h2
    else:
        _store_row_tiles(h2_ref, h2)

    logits = jnp.dot(h2.astype(BF16), wr_ref[...], preferred_element_type=F32) + br_ref[...]
    lane = lax.broadcasted_iota(jnp.int32, logits.shape, 1).astype(F32)
    big = float(LANES)
    is_group = lane < n_groups
    gl = jnp.where(is_group, logits, -jnp.inf)
    gmax = jnp.max(gl, axis=1, keepdims=True)
    g_sel = jnp.min(jnp.where(gl == gmax, lane, big), axis=1, keepdims=True)
    p_g = 1.0 / jnp.sum(jnp.where(is_group, jnp.exp(logits - gmax), 0.0), axis=1, keepdims=True)
    lo = n_groups + g_sel * experts_per_group
    el = jnp.where((lane >= lo) & (lane < lo + experts_per_group), logits, -jnp.inf)
    v1 = jnp.max(el, axis=1, keepdims=True)
    i1 = jnp.min(jnp.where(el == v1, lane, big), axis=1, keepdims=True)
    el2 = jnp.where(lane == i1, -jnp.inf, el)
    v2 = jnp.max(el2, axis=1, keepdims=True)
    i2 = jnp.min(jnp.where(el2 == v2, lane, big), axis=1, keepdims=True)
    t = jnp.exp(v2 - v1)
    w1 = p_g / (1.0 + t)
    w2 = p_g * t / (1.0 + t)
    route_ref[...] = jnp.where(lane == 0, i1 - n_groups,
                     jnp.where(lane == 1, i2 - n_groups,
                     jnp.where(lane == 2, w1,
                     jnp.where(lane == 3, w2, 0.0))))


def _mix_merge(x, o_att, weights, mix_w, mix_b, *, rows, chunk, n_sgu_groups, n_groups,
               experts_per_group, prompt, name):
    t, d = x.shape
    width = weights["ln_g"].shape[1]
    row_spec = lambda w: pl.BlockSpec((rows, w), lambda i: (i, 0))
    full = lambda a: pl.BlockSpec(a.shape, lambda i: (0,) * a.ndim)
    names = ["g1", "w2", "ln_g", "ln_b"]
    tail = ["wa", "ws", "wo", "g2", "wr", "br"]
    ins = [x, o_att] + [weights[n] for n in names] + [mix_w, mix_b] + [weights[n] for n in tail]
    in_specs = [row_spec(d), row_spec(o_att.shape[1])] + [full(a) for a in ins[2:]]
    k = d // LANES
    h2_spec = pl.BlockSpec((rows * k, LANES), lambda i: (i, 0)) if prompt else row_spec(d)
    h2_shape = (t * k, LANES) if prompt else (t, d)
    out_specs = [row_spec(d), h2_spec, row_spec(LANES)]
    out_shape = [jax.ShapeDtypeStruct((t, d), F32), jax.ShapeDtypeStruct(h2_shape, F32),
                 jax.ShapeDtypeStruct((t, LANES), F32)]
    if not prompt:
        out_specs.append(row_spec(width))
        out_shape.append(jax.ShapeDtypeStruct((t, width), F32))
    kern = functools.partial(_mix_kernel, width=width, n_sgu_groups=n_sgu_groups, chunk=chunk,
                             n_groups=n_groups, experts_per_group=experts_per_group)
    return pl.pallas_call(
        kern, grid=(t // rows,), in_specs=in_specs, out_specs=out_specs, out_shape=out_shape,
        compiler_params=_params(1), name=name,
    )(*ins)


def _swiglu(xb, wg, wu, wd):
    gate = jnp.dot(xb, wg, preferred_element_type=F32)
    up = jnp.dot(xb, wu, preferred_element_type=F32)
    act = (gate * jax.nn.sigmoid(gate)) * up
    return jnp.dot(act.astype(BF16), wd, preferred_element_type=F32)


def _expert_kernel(blk_e_ref, n_active_ref, tok_ref, tok_next_ref, dst_ref, h_hbm, wg_ref,
                   wu_ref, wd_ref, y_hbm, xbuf, ybuf, wg_bf, wu_bf, wd_bf, gsem, ssem,
                   *, rows, n_blocks):
    j = pl.program_id(0)
    n_active = n_active_ref[0]
    slot = lax.rem(j, 2)
    k = xbuf.shape[1] // rows

    def start_gather(ids_ref, s):
        for r in range(rows):
            src = pl.multiple_of(ids_ref[0, 0, r], k)
            pltpu.make_async_copy(h_hbm.at[pl.ds(src, k)], xbuf.at[s, pl.ds(r * k, k)],
                                  gsem.at[s]).start()

    def wait_gather(s):
        pltpu.make_async_copy(h_hbm.at[pl.ds(0, rows * k)], xbuf.at[s], gsem.at[s]).wait()

    def wait_scatter(s):
        pltpu.make_async_copy(ybuf.at[s], y_hbm.at[pl.ds(0, rows * k)], ssem.at[s]).wait()

    @pl.when(j == 0)
    def _prime():
        start_gather(tok_ref, 0)
        ybuf[...] = jnp.zeros(ybuf.shape, F32)
        n_real = y_hbm.shape[0] - 2 * rows * k
        for s in range(2):
            spare = pltpu.make_async_copy(
                ybuf.at[s], y_hbm.at[pl.ds(n_real + s * rows * k, rows * k)], ssem.at[s])
            spare.start()
            spare.wait()

    @pl.when(j < n_active)
    def _block():
        wait_gather(slot)

        @pl.when(j >= 2)
        def _():
            wait_scatter(slot)

        start_gather(tok_next_ref, 1 - slot)

        @pl.when((j == 0) | (blk_e_ref[j] != blk_e_ref[jnp.maximum(j - 1, 0)]))
        def _():
            wg_bf[...] = wg_ref[0].astype(BF16)
            wu_bf[...] = wu_ref[0].astype(BF16)
            wd_bf[...] = wd_ref[0].astype(BF16)

        x = _load_row_tiles(xbuf.at[slot], rows).astype(BF16)
        _store_row_tiles(ybuf.at[slot], _swiglu(x, wg_bf[...], wu_bf[...], wd_bf[...]))
        for r in range(rows):
            dst = pl.multiple_of(dst_ref[0, 0, r], k)
            pltpu.make_async_copy(ybuf.at[slot, pl.ds(r * k, k)], y_hbm.at[pl.ds(dst, k)],
                                  ssem.at[slot]).start()

    @pl.when(j == n_blocks - 1)
    def _drain():
        wait_gather(lax.rem(n_active, 2))

        @pl.when(n_active >= 1)
        def _():
            wait_scatter(lax.rem(n_active - 1, 2))

        @pl.when(n_active >= 2)
        def _():
            wait_scatter(lax.rem(n_active, 2))


def _expert_mlp(h2, plan, w_gate, w_up, w_down, *, rows, name):
    blk_e, n_active, tok, dst, n_blocks = plan
    d_expert, d = w_down.shape[1:]
    k = d // LANES
    t = h2.shape[0] // k
    smem = lambda imap: pl.BlockSpec((1, 1, rows), imap, memory_space=pltpu.SMEM)
    kern = functools.partial(_expert_kernel, rows=rows, n_blocks=n_blocks)
    return pl.pallas_call(
        kern,
        grid_spec=pltpu.PrefetchScalarGridSpec(
            num_scalar_prefetch=2,
            grid=(n_blocks,),
            in_specs=[smem(lambda j, be, na: (j, 0, 0)),
                      smem(lambda j, be, na: (jnp.minimum(j + 1, n_blocks - 1), 0, 0)),
                      smem(lambda j, be, na: (j, 0, 0)),
                      pl.BlockSpec(memory_space=pl.ANY),
                      pl.BlockSpec((1, d, d_expert), lambda j, be, na: (be[j], 0, 0)),
                      pl.BlockSpec((1, d, d_expert), lambda j, be, na: (be[j], 0, 0)),
                      pl.BlockSpec((1, d_expert, d), lambda j, be, na: (be[j], 0, 0))],
            out_specs=pl.BlockSpec(memory_space=pl.ANY),
            scratch_shapes=[pltpu.VMEM((2, rows * k, LANES), F32),
                            pltpu.VMEM((2, rows * k, LANES), F32),
                            pltpu.VMEM((d, d_expert), BF16), pltpu.VMEM((d, d_expert), BF16),
                            pltpu.VMEM((d_expert, d), BF16),
                            pltpu.SemaphoreType.DMA((2,)), pltpu.SemaphoreType.DMA((2,))]),
        out_shape=jax.ShapeDtypeStruct(((TOP_K * t + 2 * rows) * k, LANES), F32),
        compiler_params=_params(1),
        name=name,
    )(blk_e, n_active, tok, tok, dst, h2, w_gate, w_up, w_down)


def _dispatch_plan(route, n_experts, rows, k):
    t = route.shape[0]
    tk = TOP_K * t
    flat_e = route[:, :TOP_K].astype(jnp.int32).reshape(tk)
    order = jnp.argsort(flat_e, stable=True).astype(jnp.int32)
    experts = jnp.arange(n_experts, dtype=jnp.int32)
    counts = jnp.sum((flat_e[:, None] == experts[None, :]).astype(jnp.int32), axis=0)
    blocks_e = (counts + rows - 1) // rows
    blk_end = jnp.cumsum(blocks_e)
    blk_start = blk_end - blocks_e
    start = jnp.cumsum(counts) - counts
    n_blocks = (tk + n_experts * (rows - 1) + rows - 1) // rows
    j = jnp.arange(n_blocks, dtype=jnp.int32)
    blk_e = jnp.minimum(jnp.sum((j[:, None] >= blk_end[None, :]).astype(jnp.int32), axis=1),
                        n_experts - 1)
    within = j - blk_start[blk_e]
    nvalid = jnp.clip(counts[blk_e] - within * rows, 0, rows)
    r = jnp.arange(rows, dtype=jnp.int32)
    valid = r[None, :] < nvalid[:, None]
    src = jnp.clip(start[blk_e][:, None] + within[:, None] * rows + r[None, :], 0, tk - 1)
    flat = order[src]
    tok = flat // TOP_K
    spare = tk + (j & 1)[:, None] * rows + r[None, :]
    dst = (jnp.where(valid, (flat % TOP_K) * t + tok, spare) * k).reshape(n_blocks, 1, rows)
    tok = (jnp.where(valid, tok, 0) * k).reshape(n_blocks, 1, rows)
    n_active = blk_end[-1:].astype(jnp.int32)
    return blk_e, n_active, tok, dst, n_blocks


def _combine_kernel(x1_ref, y0_ref, y1_ref, route_ref, *rest):
    o_ref = rest[-1]
    route = route_ref[...]
    rows = route.shape[0]
    out = x1_ref[...] + (_load_row_tiles(y0_ref, rows) * route[:, 2:3]
                         + _load_row_tiles(y1_ref, rows) * route[:, 3:4])
    if len(rest) == 2:
        out = _rms(out, rest[0][...])
    o_ref[...] = out


def _combine(x1, y, route, final_g, *, rows, name):
    t, d = x1.shape
    k = d // LANES
    row_spec = lambda w: pl.BlockSpec((rows, w), lambda i: (i, 0))
    gains = [] if final_g is None else [final_g]
    return pl.pallas_call(
        _combine_kernel,
        grid=(t // rows,),
        in_specs=[row_spec(d), pl.BlockSpec((rows * k, LANES), lambda i: (i, 0)),
                  pl.BlockSpec((rows * k, LANES), lambda i: (i + t // rows, 0)),
                  row_spec(LANES)]
                 + [pl.BlockSpec(g.shape, lambda i: (0, 0)) for g in gains],
        out_specs=row_spec(d),
        out_shape=jax.ShapeDtypeStruct((t, d), F32),
        compiler_params=_params(1),
        name=name,
    )(x1, y, y, route, *gains)


def _dense_moe_kernel(x1_ref, h2_ref, route_ref, wg_ref, wu_ref, wd_ref, *rest):
    o_ref, acc_ref = rest[-2:]
    e = pl.program_id(0)

    @pl.when(e == 0)
    def _():
        acc_ref[...] = jnp.zeros(acc_ref.shape, F32)

    route = route_ref[...]
    ef = e.astype(F32)
    coef = (jnp.where(route[:, 0:1] == ef, route[:, 2:3], 0.0)
            + jnp.where(route[:, 1:2] == ef, route[:, 3:4], 0.0))
    routed = (route[:, 0:1] == ef) | (route[:, 1:2] == ef)
    y = _swiglu(h2_ref[...].astype(BF16), wg_ref[0].astype(BF16), wu_ref[0].astype(BF16),
                wd_ref[0].astype(BF16))
    acc_ref[...] += jnp.where(routed, y * coef, 0.0)

    @pl.when(e == pl.num_programs(0) - 1)
    def _():
        out = x1_ref[...] + acc_ref[...]
        if len(rest) == 3:
            out = _rms(out, rest[0][...])
        o_ref[...] = out


def _dense_moe(x1, h2, route, w_gate, w_up, w_down, final_g, *, name):
    t, d = x1.shape
    n_experts, d_expert = w_down.shape[:2]
    full = lambda a: pl.BlockSpec(a.shape, lambda e: (0,) * a.ndim)
    gains = [] if final_g is None else [final_g]
    return pl.pallas_call(
        _dense_moe_kernel,
        grid=(n_experts,),
        in_specs=[full(x1), full(h2), full(route),
                  pl.BlockSpec((1, d, d_expert), lambda e: (e, 0, 0)),
                  pl.BlockSpec((1, d, d_expert), lambda e: (e, 0, 0)),
                  pl.BlockSpec((1, d_expert, d), lambda e: (e, 0, 0))] + [full(g) for g in gains],
        out_specs=full(x1),
        out_shape=jax.ShapeDtypeStruct((t, d), F32),
        scratch_shapes=[pltpu.VMEM((t, d), F32)],
        compiler_params=_params(1),
        name=name,
    )(x1, h2, route, w_gate, w_up, w_down, *gains)


def _row_tile(t, want):
    return want if t % want == 0 else t


def kernel(x_prompt, x_sample, cache_k, cache_v, page_table, norm1_g, w_in, lambda_q1, lambda_k1, lambda_q2, lambda_k2, subln_g, sgu_ln_g, sgu_ln_b, sgu_w, sgu_b, w_att_branch, w_sgu_branch, w_out, norm2_g, w_router_group, b_router_group, w_router_expert, b_router_expert, w_expert_gate, w_expert_up, w_expert_down, final_g):
    batch, seq, d = x_prompt.shape
    dec_batch, ds, _ = x_sample.shape
    depth, n_phys, page, n_maps, head_dim = cache_k.shape
    n_heads = n_maps // 2
    v_dim = 2 * head_dim
    qk_width = n_maps * head_dim
    att_width = n_heads * v_dim
    width = sgu_ln_g.shape[-1]
    n_sgu_groups, chunk = sgu_w.shape[1], sgu_w.shape[2]
    n_groups = w_router_group.shape[-1]
    experts_per_group = w_router_expert.shape[-1]
    n_experts = n_groups * experts_per_group
    n_pages = page_table.shape[1]
    past = n_pages * page
    assert 2 * qk_width + att_width + 2 * width + 2 * d == w_in.shape[-1]
    assert v_dim == LANES and seq % chunk == 0 and ds <= chunk and ds <= page
    assert n_groups + n_experts <= LANES

    tp, ts = batch * seq, dec_batch * ds
    xp = x_prompt.reshape(tp, d)
    xs = x_sample.reshape(ts, d)
    tab_p = _rope_tables(jnp.arange(seq, dtype=F32), head_dim)
    tab_s = _rope_tables(jnp.arange(ds, dtype=F32) + float(past), head_dim)
    tab_s = tuple(jnp.tile(a, (dec_batch, 1)) for a in tab_s)
    ckt = cache_k.transpose(0, 1, 3, 4, 2).reshape(depth * n_phys, qk_width, page)
    cv = cache_v.reshape(depth * n_phys, page * n_heads, v_dim)
    row2 = lambda a: a.reshape(1, -1)

    proj_rows_p = _row_tile(seq, PROJ_ROWS)
    mix_rows_p = _row_tile(seq, MIX_ROWS)
    reps = ts // ds
    eye = jnp.eye(reps, dtype=F32)

    kp_l, vp_l, ks_l, vs_l, gs_l = [], [], [], [], []
    for layer in range(depth):
        lam_init = 0.8 - 0.6 * math.exp(-0.3 * layer)
        w_l = w_in[layer].astype(BF16)
        w_qkv = w_l[:, :2 * qk_width + att_width]
        lam_vecs = [row2(a[layer]) for a in (lambda_q1, lambda_k1, lambda_q2, lambda_k2)]
        g_sub = row2(subln_g[layer])
        tril = jnp.tril(sgu_w[layer])
        w_route = jnp.concatenate(
            [w_router_group[layer],
             jnp.transpose(w_router_expert[layer], (1, 0, 2)).reshape(d, n_experts)], axis=1)
        b_route = jnp.concatenate([b_router_group[layer], b_router_expert[layer].reshape(-1)])
        lane_pad = LANES - (n_groups + n_experts)
        weights = dict(
            g1=row2(norm1_g[layer]), w2=w_l[:, 2 * qk_width + att_width:],
            ln_g=row2(sgu_ln_g[layer]), ln_b=row2(sgu_ln_b[layer]),
            wa=w_att_branch[layer].astype(BF16), ws=w_sgu_branch[layer].astype(BF16),
            wo=w_out[layer].astype(BF16), g2=row2(norm2_g[layer]),
            wr=jnp.pad(w_route, ((0, 0), (0, lane_pad))).astype(BF16),
            br=row2(jnp.pad(b_route, (0, lane_pad))))
        gw = width // n_sgu_groups
        mix_w_p = tril.astype(BF16)
        mix_b_p = jnp.repeat(sgu_b[layer].T, gw, axis=1)
        mix_w_s = jnp.einsum("ab,gts->gatbs", eye, tril[:, :ds, :ds]).reshape(
            n_sgu_groups, ts, ts).astype(BF16)
        mix_b_s = jnp.tile(jnp.repeat(sgu_b[layer][:, :ds].T, gw, axis=1), (reps, 1))
        w_experts = (w_expert_gate[layer], w_expert_up[layer], w_expert_down[layer])
        g_fin = row2(final_g) if layer == depth - 1 else None

        kt_p, v_p, qb, kb, vb = _qkv_proj(
            xp, weights["g1"], w_qkv, tab_p, rows=proj_rows_p, tiles_per_seq=seq // proj_rows_p,
            n_maps=n_maps, head_dim=head_dim, prompt=True, name="qkv_prompt")
        o_att_p = _prompt_attention(qb, kb, vb, lam_vecs, g_sub, batch=batch, seq=seq,
                                    n_heads=n_heads, head_dim=head_dim, lam_init=lam_init,
                                    name="attn_prompt")
        x1_p, h2_p, route_p = _mix_merge(
            xp, o_att_p, weights, mix_w_p, mix_b_p, rows=mix_rows_p, chunk=chunk,
            n_sgu_groups=n_sgu_groups, n_groups=n_groups, experts_per_group=experts_per_group,
            prompt=True, name="mix_prompt")
        plan = _dispatch_plan(route_p, n_experts, MOE_ROWS, d // LANES)
        y_p = _expert_mlp(h2_p, plan, *w_experts, rows=MOE_ROWS, name="experts_prompt")
        xp = _combine(x1_p, y_p, route_p, g_fin, rows=_row_tile(tp, OUT_ROWS),
                      name="combine_prompt")

        k_s, v_s, q_s = _qkv_proj(
            xs, weights["g1"], w_qkv, tab_s, rows=ts, tiles_per_seq=1, n_maps=n_maps,
            head_dim=head_dim, prompt=False, name="qkv_sample")
        page_ids = (page_table.astype(jnp.int32) + layer * n_phys).reshape(-1)
        o_att_s = _decode_attention(q_s, k_s, v_s, ckt, cv, page_ids, lam_vecs, g_sub,
                                    dec_batch=dec_batch, ds=ds, n_pages=n_pages, n_maps=n_maps,
                                    lam_init=lam_init, name="attn_sample")
        x1_s, h2_s, route_s, vn_s = _mix_merge(
            xs, o_att_s, weights, mix_w_s, mix_b_s, rows=ts, chunk=ts,
            n_sgu_groups=n_sgu_groups, n_groups=n_groups, experts_per_group=experts_per_group,
            prompt=False, name="mix_sample")
        xs = _dense_moe(x1_s, h2_s, route_s, *w_experts, g_fin, name="moe_sample")

        kp_l.append(kt_p.reshape(batch, n_maps, head_dim, seq).transpose(0, 3, 1, 2))
        vp_l.append(v_p.reshape(batch, seq, n_heads, v_dim))
        ks_l.append(k_s.reshape(dec_batch, ds, n_maps, head_dim))
        vs_l.append(v_s.reshape(dec_batch, ds, n_heads, v_dim))
        gs_l.append(vn_s.reshape(dec_batch, ds, width))

    return (xp.reshape(batch, seq, d), xs.reshape(dec_batch, ds, d),
            jnp.stack(kp_l), jnp.stack(vp_l), jnp.stack(ks_l), jnp.stack(vs_l),
            jnp.stack(gs_l))
```

```python
import functools
import math

import jax
import jax.numpy as jnp
from jax import lax
from jax.experimental import pallas as pl
from jax.experimental.pallas import tpu as pltpu

F32 = jnp.float32
BF16 = jnp.bfloat16

EPS = 1e-6
ROPE_THETA = 10000.0
TOP_K = 2
LOG2E = math.log2(math.e)
LANES = 128
SUBLANES = 8
VMEM_LIMIT_BYTES = 56 * 1024 * 1024

PROJ_ROWS = 512
MIX_ROWS = 256
ATT_BLOCK = 2048
ATT_ROW_CHUNK = 512
PAGES_PER_STEP = 8
MOE_ROWS = 256
OUT_ROWS = 512


def _params(n_axes):
    return pltpu.CompilerParams(dimension_semantics=("arbitrary",) * n_axes,
                                vmem_limit_bytes=VMEM_LIMIT_BYTES)


def _rms(x, g):
    return (x * lax.rsqrt(jnp.mean(x * x, axis=-1, keepdims=True) + EPS)) * g


def _log2(n):
    assert n > 0 and n & (n - 1) == 0, n
    return n.bit_length() - 1


def _shr(x, n):
    return lax.shift_right_logical(x, _log2(n))


def _rope_tables(pos, head_dim):
    inv = 1.0 / (ROPE_THETA ** (jnp.arange(0, head_dim, 2, dtype=F32) / head_dim))
    ang = pos[:, None] * inv[None, :]
    cos, sin = jnp.cos(ang), jnp.sin(ang)
    zero = jnp.zeros_like(sin)
    reps = LANES // head_dim
    cos_t = jnp.concatenate([cos, cos] * reps, axis=-1)
    sin_lo = jnp.concatenate([-sin, zero] * reps, axis=-1)
    sin_hi = jnp.concatenate([zero, sin] * reps, axis=-1)
    return cos_t, sin_lo, sin_hi


def _qkv_kernel(x_ref, g_ref, w_ref, cos_ref, slo_ref, shi_ref, kout_ref, vout_ref, q_ref,
                *maybe_bf16_refs, qk_width, head_dim, n_maps, n_heads, q_scale, keys_transposed):
    rows = x_ref.shape[0]
    half = head_dim // 2
    h = _rms(x_ref[...], g_ref[...])
    z = jnp.dot(h.astype(BF16), w_ref[...], preferred_element_type=F32)
    cos, slo, shi = cos_ref[...], slo_ref[...], shi_ref[...]

    def rope(blk):
        return (blk * cos + pltpu.roll(blk, LANES - half, 1) * slo
                + pltpu.roll(blk, half, 1) * shi)

    maps_per_block = LANES // head_dim
    for j in range(qk_width // LANES):
        cols = slice(j * LANES, (j + 1) * LANES)
        q_ref[:, cols] = (rope(z[:, cols]) * q_scale).astype(q_ref.dtype)
        kr = rope(z[:, qk_width + j * LANES: qk_width + (j + 1) * LANES])
        if keys_transposed:
            kout_ref[0, cols, :] = kr.T
            maybe_bf16_refs[0][:, cols] = kr.astype(BF16)
        else:
            for i in range(maps_per_block):
                part = kr if i == 0 else pltpu.roll(kr, LANES - i * head_dim, 1)
                kout_ref[pl.ds(j * maps_per_block + i, rows, stride=n_maps), :] = part[:, :head_dim]
    v = z[:, 2 * qk_width:]
    if keys_transposed:
        maybe_bf16_refs[1][...] = v.astype(BF16)
    for hd in range(n_heads):
        vout_ref[pl.ds(hd, rows, stride=n_heads), :] = v[:, hd * LANES:(hd + 1) * LANES]


def _qkv_proj(x, g, w_qkv, tables, *, rows, tiles_per_seq, n_maps, head_dim, prompt, name):
    t, d = x.shape
    n_heads = n_maps // 2
    qk_width = n_maps * head_dim
    att_width = n_heads * 2 * head_dim
    seq = rows * tiles_per_seq
    row_spec = lambda w: pl.BlockSpec((rows, w), lambda i: (i, 0))
    tab_spec = pl.BlockSpec((rows, LANES), lambda i: (i % tiles_per_seq, 0))
    full = lambda a: pl.BlockSpec(a.shape, lambda i: (0,) * a.ndim)
    v_spec = pl.BlockSpec((rows * n_heads, 2 * head_dim), lambda i: (i, 0))
    v_shape = jax.ShapeDtypeStruct((t * n_heads, 2 * head_dim), F32)
    if prompt:
        kt_spec = pl.BlockSpec((1, qk_width, rows),
                               lambda i: (i // tiles_per_seq, 0, i % tiles_per_seq))
        out_specs = [kt_spec, v_spec, row_spec(qk_width), row_spec(qk_width), row_spec(att_width)]
        out_shape = [jax.ShapeDtypeStruct((t // seq, qk_width, seq), F32), v_shape,
                     jax.ShapeDtypeStruct((t, qk_width), BF16),
                     jax.ShapeDtypeStruct((t, qk_width), BF16),
                     jax.ShapeDtypeStruct((t, att_width), BF16)]
    else:
        out_specs = [pl.BlockSpec((rows * n_maps, head_dim), lambda i: (i, 0)), v_spec,
                     row_spec(qk_width)]
        out_shape = [jax.ShapeDtypeStruct((t * n_maps, head_dim), F32), v_shape,
                     jax.ShapeDtypeStruct((t, qk_width), F32)]
    kern = functools.partial(_qkv_kernel, qk_width=qk_width, head_dim=head_dim, n_maps=n_maps,
                             n_heads=n_heads, q_scale=head_dim ** -0.5 * (LOG2E if prompt else 1.0),
                             keys_transposed=prompt)
    return pl.pallas_call(
        kern,
        grid=(t // rows,),
        in_specs=[row_spec(d), full(g), full(w_qkv), tab_spec, tab_spec, tab_spec],
        out_specs=out_specs,
        out_shape=out_shape,
        compiler_params=_params(1),
        name=name,
    )(x, g, w_qkv, *tables)


def _diff_lambda(lq1_ref, lk1_ref, lq2_ref, lk2_ref, lam_init):
    s1 = jnp.sum(lq1_ref[...] * lk1_ref[...], axis=-1, keepdims=True)
    s2 = jnp.sum(lq2_ref[...] * lk2_ref[...], axis=-1, keepdims=True)
    return jnp.exp(s1) - jnp.exp(s2) + lam_init


def _subln(o1, o2, lam, g, lam_init):
    d = o1 - lam * o2
    return _rms(d, g) * (1.0 - lam_init)


def _prompt_attn_kernel(qi_ref, ki_ref, q_ref, k_ref, v_ref, lq1_ref, lk1_ref, lq2_ref,
                        lk2_ref, g_ref, o_ref, q2_ref, m_ref, acc_ref,
                        *, blk, chunk, head_dim, lam_init):
    p = pl.program_id(2)
    qi, ki = qi_ref[p], ki_ref[p]
    v_dim = 2 * head_dim

    @pl.when(ki == 0)
    def _init():
        q = q_ref[...]
        lane = lax.broadcasted_iota(jnp.int32, q.shape, 1)
        zero = jnp.zeros_like(q)
        q2_ref[:blk, :] = jnp.where(lane < head_dim, q, zero)
        q2_ref[blk:, :] = jnp.where(lane >= head_dim, q, zero)
        m_ref[...] = jnp.full(m_ref.shape, -jnp.inf, F32)
        acc_ref[...] = jnp.zeros(acc_ref.shape, F32)

    def step(masked):
        for c in range(2 * blk // chunk):
            rows = slice(c * chunk, (c + 1) * chunk)
            first = (c * chunk) % blk
            n_keys = first + chunk if masked else blk
            v1 = jnp.concatenate([v_ref[:n_keys, :], jnp.ones((n_keys, LANES), BF16)], axis=1)
            s = lax.dot_general(q2_ref[rows, :], k_ref[:n_keys, :], (((1,), (1,)), ((), ())),
                                preferred_element_type=F32)
            if masked:
                r = lax.broadcasted_iota(jnp.int32, s.shape, 0) + first
                col = lax.broadcasted_iota(jnp.int32, s.shape, 1)
                s = jnp.where(col <= r, s, -jnp.inf)
            m_prev = m_ref[rows, :]
            m_new = jnp.maximum(m_prev, jnp.max(s, axis=1, keepdims=True))
            alpha = jnp.exp2(m_prev - m_new)
            pexp = jnp.exp2(s - jnp.concatenate([m_new] * (n_keys // LANES), axis=1))
            acc_ref[rows, :] = (jnp.concatenate([alpha] * (acc_ref.shape[1] // LANES), axis=1)
                                * acc_ref[rows, :]
                                + jnp.dot(pexp.astype(BF16), v1, preferred_element_type=F32))
            m_ref[rows, :] = m_new

    @pl.when(ki < qi)
    def _off_diagonal():
        step(False)

    @pl.when(ki == qi)
    def _diagonal():
        step(True)
        acc = acc_ref[...]
        o = acc[:, :v_dim] / acc[:, v_dim:]
        lam = _diff_lambda(lq1_ref, lk1_ref, lq2_ref, lk2_ref, lam_init)
        o_ref[...] = _subln(o[:blk], o[blk:], lam, g_ref[...], lam_init).astype(o_ref.dtype)


def _prompt_attention(qb, kb, vb, lam_vecs, subln_g, *, batch, seq, n_heads, head_dim,
                      lam_init, name):
    blk = min(ATT_BLOCK, seq)
    chunk = min(ATT_ROW_CHUNK, blk)
    nq = seq // blk
    pairs = [(i, j) for i in range(nq) for j in range(i + 1)]
    qi_map = jnp.asarray([a for a, _ in pairs], jnp.int32)
    ki_map = jnp.asarray([b for _, b in pairs], jnp.int32)
    v_dim = 2 * head_dim
    assert v_dim == LANES
    q_spec = pl.BlockSpec((blk, v_dim), lambda b, h, p, qi, ki: (b * nq + qi[p], h))
    kv_spec = pl.BlockSpec((blk, v_dim), lambda b, h, p, qi, ki: (b * nq + ki[p], h))
    vec = lambda a: pl.BlockSpec(a.shape, lambda b, h, p, qi, ki: (0, 0))
    kern = functools.partial(_prompt_attn_kernel, blk=blk, chunk=chunk, head_dim=head_dim,
                             lam_init=lam_init)
    return pl.pallas_call(
        kern,
        grid_spec=pltpu.PrefetchScalarGridSpec(
            num_scalar_prefetch=2,
            grid=(batch, n_heads, len(pairs)),
            in_specs=[q_spec, kv_spec, kv_spec] + [vec(a) for a in lam_vecs] + [vec(subln_g)],
            out_specs=q_spec,
            scratch_shapes=[pltpu.VMEM((2 * blk, v_dim), BF16),
                            pltpu.VMEM((2 * blk, LANES), F32),
                            pltpu.VMEM((2 * blk, v_dim + LANES), F32)]),
        out_shape=jax.ShapeDtypeStruct((batch * seq, n_heads * v_dim), BF16),
        compiler_params=_params(3),
        name=name,
    )(qi_map, ki_map, qb, kb, vb, *lam_vecs, subln_g)


def _decode_attn_kernel(*refs, n_pages_step, page, ds, n_maps, head_dim, lam_init):
    pps = n_pages_step
    q_ref, knew_ref, vnew_ref = refs[1:4]
    k_refs = refs[4:4 + pps]
    v_refs = refs[4 + pps:4 + 2 * pps]
    lq1_ref, lk1_ref, lq2_ref, lk2_ref, g_ref = refs[4 + 2 * pps:9 + 2 * pps]
    o_ref = refs[9 + 2 * pps]
    qs_ref, m_ref, l_ref, acc_ref = refs[10 + 2 * pps:]
    j = pl.program_id(1)
    n_heads = n_maps // 2
    v_dim = 2 * head_dim
    head_rows = 2 * ds

    def head_values(v_ref, h):
        return v_ref[0, pl.ds(h, page, stride=n_heads), :].astype(BF16)

    def update(keys, values, causal):
        s = jnp.dot(qs_ref[...], keys, preferred_element_type=F32)
        if causal:
            qpos = lax.broadcasted_iota(jnp.int32, s.shape, 0) & (ds - 1)
            kpos = lax.broadcasted_iota(jnp.int32, s.shape, 1)
            s = jnp.where(kpos <= qpos, s, -jnp.inf)
        m_prev = m_ref[...]
        m_new = jnp.maximum(m_prev, jnp.max(s, axis=1, keepdims=True))
        alpha = jnp.exp(m_prev - m_new)
        pexp = jnp.exp(s - jnp.concatenate([m_new] * (s.shape[1] // LANES), axis=1))
        l_ref[...] = alpha * l_ref[...] + jnp.sum(pexp, axis=1, keepdims=True)
        p_hi = pexp.astype(BF16)
        p_lo = (pexp - p_hi.astype(F32)).astype(BF16)
        for h in range(n_heads):
            r = slice(h * head_rows, (h + 1) * head_rows)
            pv = jnp.dot(jnp.concatenate([p_hi[r], p_lo[r]], axis=0), values[h],
                         preferred_element_type=F32)
            acc_ref[r, :] = alpha[r] * acc_ref[r, :] + (pv[:head_rows] + pv[head_rows:])
        m_ref[...] = m_new

    @pl.when(j == 0)
    def _init():
        q = q_ref[...]
        qt = jnp.concatenate([q] * n_maps, axis=0)
        r = lax.broadcasted_iota(jnp.int32, qt.shape, 0)
        c = lax.broadcasted_iota(jnp.int32, qt.shape, 1)
        qs_ref[...] = jnp.where(_shr(c, head_dim) == _shr(r, ds), qt,
                                jnp.zeros_like(qt)).astype(BF16)
        m_ref[...] = jnp.full(m_ref.shape, -jnp.inf, F32)
        l_ref[...] = jnp.zeros(l_ref.shape, F32)
        acc_ref[...] = jnp.zeros(acc_ref.shape, F32)
        update(knew_ref[0].astype(BF16), [head_values(vnew_ref, h) for h in range(n_heads)],
               causal=True)

    keys = jnp.concatenate([k_refs[i][0].astype(BF16) for i in range(pps)], axis=1)
    values = [jnp.concatenate([head_values(v_refs[i], h) for i in range(pps)], axis=0)
              for h in range(n_heads)]
    update(keys, values, causal=False)

    @pl.when(j == pl.num_programs(1) - 1)
    def _finish():
        lam = _diff_lambda(lq1_ref, lk1_ref, lq2_ref, lk2_ref, lam_init)
        o = acc_ref[...] / l_ref[...]
        for h in range(n_heads):
            o1 = o[h * head_rows:h * head_rows + ds]
            o2 = o[h * head_rows + ds:(h + 1) * head_rows]
            o_ref[:, h * v_dim:(h + 1) * v_dim] = _subln(o1, o2, lam, g_ref[...],
                                                         lam_init).astype(o_ref.dtype)


def _decode_attention(q, k_new, v_new, cache_kt, cache_v, page_ids, lam_vecs, subln_g, *,
                      dec_batch, ds, n_pages, n_maps, lam_init, name):
    qk_width, page = cache_kt.shape[1:]
    v_dim = cache_v.shape[2]
    n_heads = cache_v.shape[1] // page
    head_dim = qk_width // n_maps
    rows = n_maps * ds
    assert v_dim == LANES and rows % SUBLANES == 0
    pps = math.gcd(PAGES_PER_STEP, n_pages)
    k_new = jnp.pad(k_new.reshape(dec_batch, ds, qk_width),
                    ((0, 0), (0, page - ds), (0, 0))).transpose(0, 2, 1)
    v_new = jnp.pad(v_new.reshape(dec_batch, ds * n_heads, v_dim),
                    ((0, 0), (0, (page - ds) * n_heads), (0, 0)))

    def page_spec(i, shape):
        return pl.BlockSpec((1,) + shape, lambda b, j, pt: (pt[b * n_pages + j * pps + i],)
                            + (0,) * len(shape))

    vec = lambda a: pl.BlockSpec(a.shape, lambda b, j, pt: (0, 0))
    kern = functools.partial(_decode_attn_kernel, n_pages_step=pps, page=page, ds=ds,
                             n_maps=n_maps, head_dim=head_dim, lam_init=lam_init)
    return pl.pallas_call(
        kern,
        grid_spec=pltpu.PrefetchScalarGridSpec(
            num_scalar_prefetch=1,
            grid=(dec_batch, n_pages // pps),
            in_specs=([pl.BlockSpec((ds, qk_width), lambda b, j, pt: (b, 0)),
                       pl.BlockSpec((1, qk_width, page), lambda b, j, pt: (b, 0, 0)),
                       pl.BlockSpec((1, page * n_heads, v_dim), lambda b, j, pt: (b, 0, 0))]
                      + [page_spec(i, (qk_width, page)) for i in range(pps)]
                      + [page_spec(i, (page * n_heads, v_dim)) for i in range(pps)]
                      + [vec(a) for a in lam_vecs] + [vec(subln_g)]),
            out_specs=pl.BlockSpec((ds, n_heads * v_dim), lambda b, j, pt: (b, 0)),
            scratch_shapes=[pltpu.VMEM((rows, qk_width), BF16),
                            pltpu.VMEM((rows, LANES), F32),
                            pltpu.VMEM((rows, LANES), F32),
                            pltpu.VMEM((rows, v_dim), F32)]),
        out_shape=jax.ShapeDtypeStruct((dec_batch * ds, n_heads * v_dim), F32),
        compiler_params=_params(2),
        name=name,
    )(page_ids, q, k_new, v_new, *([cache_kt] * pps), *([cache_v] * pps), *lam_vecs, subln_g)


def _store_row_tiles(ref, x):
    rows, k = x.shape[0], x.shape[1] // LANES
    for c in range(k):
        ref[pl.ds(c, rows, stride=k), :] = x[:, c * LANES:(c + 1) * LANES]


def _load_row_tiles(ref, rows):
    k = ref.shape[0] // rows
    return jnp.concatenate([ref[pl.ds(c, rows, stride=k), :] for c in range(k)], axis=1)


def _gelu_tanh(x):
    return 0.5 * x * (1.0 + jnp.tanh(math.sqrt(2.0 / math.pi) * (x + 0.044715 * (x * x * x))))


def _mix_kernel(x_ref, oatt_ref, g1_ref, w2_ref, lng_ref, lnb_ref, mixw_ref, mixb_ref,
                wa_ref, ws_ref, wo_ref, g2_ref, wr_ref, br_ref,
                x1_ref, h2_ref, route_ref, *maybe_vn_ref,
                width, n_sgu_groups, chunk, n_groups, experts_per_group):
    x = x_ref[...]
    h = _rms(x, g1_ref[...]).astype(BF16)
    z = jnp.dot(h, w2_ref[...], preferred_element_type=F32)
    u = _gelu_tanh(z[:, :width])
    vg = _gelu_tanh(z[:, width:2 * width])
    mu = jnp.mean(vg, axis=-1, keepdims=True)
    var = jnp.mean(jnp.square(vg - mu), axis=-1, keepdims=True)
    vn = (vg - mu) * lax.rsqrt(var + EPS) * lng_ref[...] + lnb_ref[...]
    if maybe_vn_ref:
        maybe_vn_ref[0][...] = vn
    vnb = vn.astype(BF16)

    rows = x.shape[0]
    gw = width // n_sgu_groups
    mixed_chunks = []
    for c in range(rows // chunk):
        parts = [jnp.dot(mixw_ref[g], vnb[c * chunk:(c + 1) * chunk, g * gw:(g + 1) * gw],
                         preferred_element_type=F32) for g in range(n_sgu_groups)]
        mixed_chunks.append(jnp.concatenate(parts, axis=1) + mixb_ref[...])
    mixed = jnp.concatenate(mixed_chunks, axis=0) if len(mixed_chunks) > 1 else mixed_chunks[0]
    o_sgu = (u * mixed).astype(BF16)

    a_br = jnp.dot(oatt_ref[...].astype(BF16), wa_ref[...], preferred_element_type=F32)
    s_br = jnp.dot(o_sgu, ws_ref[...], preferred_element_type=F32)
    d = x.shape[1]
    m = (jax.nn.sigmoid(z[:, 2 * width:2 * width + d]) * a_br
         + jax.nn.sigmoid(z[:, 2 * width + d:]) * s_br)
    x1 = x + jnp.dot(m.astype(BF16), wo_ref[...], preferred_element_type=F32)
    x1_ref[...] = x1
    h2 = _rms(x1, g2_ref[...])
    if h2_ref.shape == x1.shape:
        h2_ref[...] = h2
    else:
        _store_row_tiles(h2_ref, h2)

    logits = jnp.dot(h2.astype(BF16), wr_ref[...], preferred_element_type=F32) + br_ref[...]
    lane = lax.broadcasted_iota(jnp.int32, logits.shape, 1).astype(F32)
    big = float(LANES)
    is_group = lane < n_groups
    gl = jnp.where(is_group, logits, -jnp.inf)
    gmax = jnp.max(gl, axis=1, keepdims=True)
    g_sel = jnp.min(jnp.where(gl == gmax, lane, big), axis=1, keepdims=True)
    p_g = 1.0 / jnp.sum(jnp.where(is_group, jnp.exp(logits - gmax), 0.0), axis=1, keepdims=True)
    lo = n_groups + g_sel * experts_per_group
    el = jnp.where((lane >= lo) & (lane < lo + experts_per_group), logits, -jnp.inf)
    v1 = jnp.max(el, axis=1, keepdims=True)
    i1 = jnp.min(jnp.where(el == v1, lane, big), axis=1, keepdims=True)
    el2 = jnp.where(lane == i1, -jnp.inf, el)
    v2 = jnp.max(el2, axis=1, keepdims=True)
    i2 = jnp.min(jnp.where(el2 == v2, lane, big), axis=1, keepdims=True)
    t = jnp.exp(v2 - v1)
    w1 = p_g / (1.0 + t)
    w2 = p_g * t / (1.0 + t)
    route_ref[...] = jnp.where(lane == 0, i1 - n_groups,
                     jnp.where(lane == 1, i2 - n_groups,
                     jnp.where(lane == 2, w1,
                     jnp.where(lane == 3, w2, 0.0))))


def _mix_merge(x, o_att, weights, mix_w, mix_b, *, rows, chunk, n_sgu_groups, n_groups,
               experts_per_group, prompt, name):
    t, d = x.shape
    width = weights["ln_g"].shape[1]
    row_spec = lambda w: pl.BlockSpec((rows, w), lambda i: (i, 0))
    full = lambda a: pl.BlockSpec(a.shape, lambda i: (0,) * a.ndim)
    names = ["g1", "w2", "ln_g", "ln_b"]
    tail = ["wa", "ws", "wo", "g2", "wr", "br"]
    ins = [x, o_att] + [weights[n] for n in names] + [mix_w, mix_b] + [weights[n] for n in tail]
    in_specs = [row_spec(d), row_spec(o_att.shape[1])] + [full(a) for a in ins[2:]]
    k = d // LANES
    h2_spec = pl.BlockSpec((rows * k, LANES), lambda i: (i, 0)) if prompt else row_spec(d)
    h2_shape = (t * k, LANES) if prompt else (t, d)
    out_specs = [row_spec(d), h2_spec, row_spec(LANES)]
    out_shape = [jax.ShapeDtypeStruct((t, d), F32), jax.ShapeDtypeStruct(h2_shape, F32),
                 jax.ShapeDtypeStruct((t, LANES), F32)]
    if not prompt:
        out_specs.append(row_spec(width))
        out_shape.append(jax.ShapeDtypeStruct((t, width), F32))
    kern = functools.partial(_mix_kernel, width=width, n_sgu_groups=n_sgu_groups, chunk=chunk,
                             n_groups=n_groups, experts_per_group=experts_per_group)
    return pl.pallas_call(
        kern, grid=(t // rows,), in_specs=in_specs, out_specs=out_specs, out_shape=out_shape,
        compiler_params=_params(1), name=name,
    )(*ins)


def _swiglu(xb, wg, wu, wd):
    gate = jnp.dot(xb, wg, preferred_element_type=F32)
    up = jnp.dot(xb, wu, preferred_element_type=F32)
    act = (gate * jax.nn.sigmoid(gate)) * up
    return jnp.dot(act.astype(BF16), wd, preferred_element_type=F32)


def _expert_kernel(blk_e_ref, n_active_ref, tok_ref, tok_next_ref, dst_ref, h_hbm, wg_ref,
                   wu_ref, wd_ref, y_hbm, xbuf, ybuf, wg_bf, wu_bf, wd_bf, gsem, ssem,
                   *, rows, n_blocks):
    j = pl.program_id(0)
    n_active = n_active_ref[0]
    slot = lax.rem(j, 2)
    k = xbuf.shape[1] // rows

    def start_gather(ids_ref, s):
        for r in range(rows):
            src = pl.multiple_of(ids_ref[0, 0, r], k)
            pltpu.make_async_copy(h_hbm.at[pl.ds(src, k)], xbuf.at[s, pl.ds(r * k, k)],
                                  gsem.at[s]).start(priority=r % 2)

    def wait_gather(s):
        pltpu.make_async_copy(h_hbm.at[pl.ds(0, rows * k)], xbuf.at[s], gsem.at[s]).wait()

    def wait_scatter(s):
        pltpu.make_async_copy(ybuf.at[s], y_hbm.at[pl.ds(0, rows * k)], ssem.at[s]).wait()

    @pl.when(j == 0)
    def _prime():
        start_gather(tok_ref, 0)
        ybuf[...] = jnp.zeros(ybuf.shape, F32)
        n_real = y_hbm.shape[0] - 2 * rows * k
        for s in range(2):
            spare = pltpu.make_async_copy(
                ybuf.at[s], y_hbm.at[pl.ds(n_real + s * rows * k, rows * k)], ssem.at[s])
            spare.start()
            spare.wait()

    @pl.when(j < n_active)
    def _block():
        wait_gather(slot)

        @pl.when(j >= 2)
        def _():
            wait_scatter(slot)

        start_gather(tok_next_ref, 1 - slot)

        @pl.when((j == 0) | (blk_e_ref[j] != blk_e_ref[jnp.maximum(j - 1, 0)]))
        def _():
            wg_bf[...] = wg_ref[0].astype(BF16)
            wu_bf[...] = wu_ref[0].astype(BF16)
            wd_bf[...] = wd_ref[0].astype(BF16)

        x = _load_row_tiles(xbuf.at[slot], rows).astype(BF16)
        _store_row_tiles(ybuf.at[slot], _swiglu(x, wg_bf[...], wu_bf[...], wd_bf[...]))
        for r in range(rows):
            dst = pl.multiple_of(dst_ref[0, 0, r], k)
            pltpu.make_async_copy(ybuf.at[slot, pl.ds(r * k, k)], y_hbm.at[pl.ds(dst, k)],
                                  ssem.at[slot]).start(priority=r % 2)

    @pl.when(j == n_blocks - 1)
    def _drain():
        wait_gather(lax.rem(n_active, 2))

        @pl.when(n_active >= 1)
        def _():
            wait_scatter(lax.rem(n_active - 1, 2))

        @pl.when(n_active >= 2)
        def _():
            wait_scatter(lax.rem(n_active, 2))


def _expert_mlp(h2, plan, w_gate, w_up, w_down, *, rows, name):
    blk_e, n_active, tok, dst, n_blocks = plan
    d_expert, d = w_down.shape[1:]
    k = d // LANES
    t = h2.shape[0] // k
    smem = lambda imap: pl.BlockSpec((1, 1, rows), imap, memory_space=pltpu.SMEM)
    kern = functools.partial(_expert_kernel, rows=rows, n_blocks=n_blocks)
    return pl.pallas_call(
        kern,
        grid_spec=pltpu.PrefetchScalarGridSpec(
            num_scalar_prefetch=2,
            grid=(n_blocks,),
            in_specs=[smem(lambda j, be, na: (j, 0, 0)),
                      smem(lambda j, be, na: (jnp.minimum(j + 1, n_blocks - 1), 0, 0)),
                      smem(lambda j, be, na: (j, 0, 0)),
                      pl.BlockSpec(memory_space=pl.ANY),
                      pl.BlockSpec((1, d, d_expert), lambda j, be, na: (be[j], 0, 0)),
                      pl.BlockSpec((1, d, d_expert), lambda j, be, na: (be[j], 0, 0)),
                      pl.BlockSpec((1, d_expert, d), lambda j, be, na: (be[j], 0, 0))],
            out_specs=pl.BlockSpec(memory_space=pl.ANY),
            scratch_shapes=[pltpu.VMEM((2, rows * k, LANES), F32),
                            pltpu.VMEM((2, rows * k, LANES), F32),
                            pltpu.VMEM((d, d_expert), BF16), pltpu.VMEM((d, d_expert), BF16),
                            pltpu.VMEM((d_expert, d), BF16),
                            pltpu.SemaphoreType.DMA((2,)), pltpu.SemaphoreType.DMA((2,))]),
        out_shape=jax.ShapeDtypeStruct(((TOP_K * t + 2 * rows) * k, LANES), F32),
        compiler_params=_params(1),
        name=name,
    )(blk_e, n_active, tok, tok, dst, h2, w_gate, w_up, w_down)


def _dispatch_plan(route, n_experts, rows, k):
    t = route.shape[0]
    tk = TOP_K * t
    flat_e = route[:, :TOP_K].astype(jnp.int32).reshape(tk)
    order = jnp.argsort(flat_e, stable=True).astype(jnp.int32)
    experts = jnp.arange(n_experts, dtype=jnp.int32)
    counts = jnp.sum((flat_e[:, None] == experts[None, :]).astype(jnp.int32), axis=0)
    blocks_e = (counts + rows - 1) // rows
    blk_end = jnp.cumsum(blocks_e)
    blk_start = blk_end - blocks_e
    start = jnp.cumsum(counts) - counts
    n_blocks = (tk + n_experts * (rows - 1) + rows - 1) // rows
    j = jnp.arange(n_blocks, dtype=jnp.int32)
    blk_e = jnp.minimum(jnp.sum((j[:, None] >= blk_end[None, :]).astype(jnp.int32), axis=1),
                        n_experts - 1)
    within = j - blk_start[blk_e]
    nvalid = jnp.clip(counts[blk_e] - within * rows, 0, rows)
    r = jnp.arange(rows, dtype=jnp.int32)
    valid = r[None, :] < nvalid[:, None]
    src = jnp.clip(start[blk_e][:, None] + within[:, None] * rows + r[None, :], 0, tk - 1)
    flat = order[src]
    tok = flat // TOP_K
    spare = tk + (j & 1)[:, None] * rows + r[None, :]
    dst = (jnp.where(valid, (flat % TOP_K) * t + tok, spare) * k).reshape(n_blocks, 1, rows)
    tok = (jnp.where(valid, tok, 0) * k).reshape(n_blocks, 1, rows)
    n_active = blk_end[-1:].astype(jnp.int32)
    return blk_e, n_active, tok, dst, n_blocks


def _combine_kernel(x1_ref, y0_ref, y1_ref, route_ref, *rest):
    o_ref = rest[-1]
    route = route_ref[...]
    rows = route.shape[0]
    out = x1_ref[...] + (_load_row_tiles(y0_ref, rows) * route[:, 2:3]
                         + _load_row_tiles(y1_ref, rows) * route[:, 3:4])
    if len(rest) == 2:
        out = _rms(out, rest[0][...])
    o_ref[...] = out


def _combine(x1, y, route, final_g, *, rows, name):
    t, d = x1.shape
    k = d // LANES
    row_spec = lambda w: pl.BlockSpec((rows, w), lambda i: (i, 0))
    gains = [] if final_g is None else [final_g]
    return pl.pallas_call(
        _combine_kernel,
        grid=(t // rows,),
        in_specs=[row_spec(d), pl.BlockSpec((rows * k, LANES), lambda i: (i, 0)),
                  pl.BlockSpec((rows * k, LANES), lambda i: (i + t // rows, 0)),
                  row_spec(LANES)]
                 + [pl.BlockSpec(g.shape, lambda i: (0, 0)) for g in gains],
        out_specs=row_spec(d),
        out_shape=jax.ShapeDtypeStruct((t, d), F32),
        compiler_params=_params(1),
        name=name,
    )(x1, y, y, route, *gains)


def _dense_moe_kernel(x1_ref, h2_ref, route_ref, wg_ref, wu_ref, wd_ref, *rest):
    o_ref, acc_ref = rest[-2:]
    e = pl.program_id(0)

    @pl.when(e == 0)
    def _():
        acc_ref[...] = jnp.zeros(acc_ref.shape, F32)

    route = route_ref[...]
    ef = e.astype(F32)
    coef = (jnp.where(route[:, 0:1] == ef, route[:, 2:3], 0.0)
            + jnp.where(route[:, 1:2] == ef, route[:, 3:4], 0.0))
    routed = (route[:, 0:1] == ef) | (route[:, 1:2] == ef)
    y = _swiglu(h2_ref[...].astype(BF16), wg_ref[0].astype(BF16), wu_ref[0].astype(BF16),
                wd_ref[0].astype(BF16))
    acc_ref[...] += jnp.where(routed, y * coef, 0.0)

    @pl.when(e == pl.num_programs(0) - 1)
    def _():
        out = x1_ref[...] + acc_ref[...]
        if len(rest) == 3:
            out = _rms(out, rest[0][...])
        o_ref[...] = out


def _dense_moe(x1, h2, route, w_gate, w_up, w_down, final_g, *, name):
    t, d = x1.shape
    n_experts, d_expert = w_down.shape[:2]
    full = lambda a: pl.BlockSpec(a.shape, lambda e: (0,) * a.ndim)
    gains = [] if final_g is None else [final_g]
    return pl.pallas_call(
        _dense_moe_kernel,
        grid=(n_experts,),
        in_specs=[full(x1), full(h2), full(route),
                  pl.BlockSpec((1, d, d_expert), lambda e: (e, 0, 0)),
                  pl.BlockSpec((1, d, d_expert), lambda e: (e, 0, 0)),
                  pl.BlockSpec((1, d_expert, d), lambda e: (e, 0, 0))] + [full(g) for g in gains],
        out_specs=full(x1),
        out_shape=jax.ShapeDtypeStruct((t, d), F32),
        scratch_shapes=[pltpu.VMEM((t, d), F32)],
        compiler_params=_params(1),
        name=name,
    )(x1, h2, route, w_gate, w_up, w_down, *gains)


def _row_tile(t, want):
    return want if t % want == 0 else t


def kernel(x_prompt, x_sample, cache_k, cache_v, page_table, norm1_g, w_in, lambda_q1, lambda_k1, lambda_q2, lambda_k2, subln_g, sgu_ln_g, sgu_ln_b, sgu_w, sgu_b, w_att_branch, w_sgu_branch, w_out, norm2_g, w_router_group, b_router_group, w_router_expert, b_router_expert, w_expert_gate, w_expert_up, w_expert_down, final_g):
    batch, seq, d = x_prompt.shape
    dec_batch, ds, _ = x_sample.shape
    depth, n_phys, page, n_maps, head_dim = cache_k.shape
    n_heads = n_maps // 2
    v_dim = 2 * head_dim
    qk_width = n_maps * head_dim
    att_width = n_heads * v_dim
    width = sgu_ln_g.shape[-1]
    n_sgu_groups, chunk = sgu_w.shape[1], sgu_w.shape[2]
    n_groups = w_router_group.shape[-1]
    experts_per_group = w_router_expert.shape[-1]
    n_experts = n_groups * experts_per_group
    n_pages = page_table.shape[1]
    past = n_pages * page
    assert 2 * qk_width + att_width + 2 * width + 2 * d == w_in.shape[-1]
    assert v_dim == LANES and seq % chunk == 0 and ds <= chunk and ds <= page
    assert n_groups + n_experts <= LANES

    tp, ts = batch * seq, dec_batch * ds
    xp = x_prompt.reshape(tp, d)
    xs = x_sample.reshape(ts, d)
    tab_p = _rope_tables(jnp.arange(seq, dtype=F32), head_dim)
    tab_s = _rope_tables(jnp.arange(ds, dtype=F32) + float(past), head_dim)
    tab_s = tuple(jnp.tile(a, (dec_batch, 1)) for a in tab_s)
    ckt = cache_k.transpose(0, 1, 3, 4, 2).reshape(depth * n_phys, qk_width, page)
    cv = cache_v.reshape(depth * n_phys, page * n_heads, v_dim)
    row2 = lambda a: a.reshape(1, -1)

    proj_rows_p = _row_tile(seq, PROJ_ROWS)
    mix_rows_p = _row_tile(seq, MIX_ROWS)
    reps = ts // ds
    eye = jnp.eye(reps, dtype=F32)

    kp_l, vp_l, ks_l, vs_l, gs_l = [], [], [], [], []
    for layer in range(depth):
        lam_init = 0.8 - 0.6 * math.exp(-0.3 * layer)
        w_l = w_in[layer].astype(BF16)
        w_qkv = w_l[:, :2 * qk_width + att_width]
        lam_vecs = [row2(a[layer]) for a in (lambda_q1, lambda_k1, lambda_q2, lambda_k2)]
        g_sub = row2(subln_g[layer])
        tril = jnp.tril(sgu_w[layer])
        w_route = jnp.concatenate(
            [w_router_group[layer],
             jnp.transpose(w_router_expert[layer], (1, 0, 2)).reshape(d, n_experts)], axis=1)
        b_route = jnp.concatenate([b_router_group[layer], b_router_expert[layer].reshape(-1)])
        lane_pad = LANES - (n_groups + n_experts)
        weights = dict(
            g1=row2(norm1_g[layer]), w2=w_l[:, 2 * qk_width + att_width:],
            ln_g=row2(sgu_ln_g[layer]), ln_b=row2(sgu_ln_b[layer]),
            wa=w_att_branch[layer].astype(BF16), ws=w_sgu_branch[layer].astype(BF16),
            wo=w_out[layer].astype(BF16), g2=row2(norm2_g[layer]),
            wr=jnp.pad(w_route, ((0, 0), (0, lane_pad))).astype(BF16),
            br=row2(jnp.pad(b_route, (0, lane_pad))))
        gw = width // n_sgu_groups
        mix_w_p = tril.astype(BF16)
        mix_b_p = jnp.repeat(sgu_b[layer].T, gw, axis=1)
        mix_w_s = jnp.einsum("ab,gts->gatbs", eye, tril[:, :ds, :ds]).reshape(
            n_sgu_groups, ts, ts).astype(BF16)
        mix_b_s = jnp.tile(jnp.repeat(sgu_b[layer][:, :ds].T, gw, axis=1), (reps, 1))
        w_experts = (w_expert_gate[layer], w_expert_up[layer], w_expert_down[layer])
        g_fin = row2(final_g) if layer == depth - 1 else None

        kt_p, v_p, qb, kb, vb = _qkv_proj(
            xp, weights["g1"], w_qkv, tab_p, rows=proj_rows_p, tiles_per_seq=seq // proj_rows_p,
            n_maps=n_maps, head_dim=head_dim, prompt=True, name="qkv_prompt")
        o_att_p = _prompt_attention(qb, kb, vb, lam_vecs, g_sub, batch=batch, seq=seq,
                                    n_heads=n_heads, head_dim=head_dim, lam_init=lam_init,
                                    name="attn_prompt")
        x1_p, h2_p, route_p = _mix_merge(
            xp, o_att_p, weights, mix_w_p, mix_b_p, rows=mix_rows_p, chunk=chunk,
            n_sgu_groups=n_sgu_groups, n_groups=n_groups, experts_per_group=experts_per_group,
            prompt=True, name="mix_prompt")
        plan = _dispatch_plan(route_p, n_experts, MOE_ROWS, d // LANES)
        y_p = _expert_mlp(h2_p, plan, *w_experts, rows=MOE_ROWS, name="experts_prompt")
        xp = _combine(x1_p, y_p, route_p, g_fin, rows=_row_tile(tp, OUT_ROWS),
                      name="combine_prompt")

        k_s, v_s, q_s = _qkv_proj(
            xs, weights["g1"], w_qkv, tab_s, rows=ts, tiles_per_seq=1, n_maps=n_maps,
            head_dim=head_dim, prompt=False, name="qkv_sample")
        page_ids = (page_table.astype(jnp.int32) + layer * n_phys).reshape(-1)
        o_att_s = _decode_attention(q_s, k_s, v_s, ckt, cv, page_ids, lam_vecs, g_sub,
                                    dec_batch=dec_batch, ds=ds, n_pages=n_pages, n_maps=n_maps,
                                    lam_init=lam_init, name="attn_sample")
        x1_s, h2_s, route_s, vn_s = _mix_merge(
            xs, o_att_s, weights, mix_w_s, mix_b_s, rows=ts, chunk=ts,
            n_sgu_groups=n_sgu_groups, n_groups=n_groups, experts_per_group=experts_per_group,
            prompt=False, name="mix_sample")
        xs = _dense_moe(x1_s, h2_s, route_s, *w_experts, g_fin, name="moe_sample")

        kp_l.append(kt_p.reshape(batch, n_maps, head_dim, seq).transpose(0, 3, 1, 2))
        vp_l.append(v_p.reshape(batch, seq, n_heads, v_dim))
        ks_l.append(k_s.reshape(dec_batch, ds, n_maps, head_dim))
        vs_l.append(v_s.reshape(dec_batch, ds, n_heads, v_dim))
        gs_l.append(vn_s.reshape(dec_batch, ds, width))

    return (xp.reshape(batch, seq, d), xs.reshape(dec_batch, ds, d),
            jnp.stack(kp_l), jnp.stack(vp_l), jnp.stack(ks_l), jnp.stack(vs_l),
            jnp.stack(gs_l))
```
